```python
import jax, jax.numpy as jnp
from jax import lax
import numpy as np

D_MODEL = 2048
BATCH = 1
SEQ = 8192
DEPTH = 2
DEC_BATCH = 16
DEC_SEQ = 32
PAST_LEN = 2048

CHUNK = 64
N_AB = (DEPTH + 1) // 2
N_C = DEPTH // 2
D_FF = 5632
EPS = 1e-6
LRU_W = D_MODEL // 2
LRU_BLOCKS = 8
LRU_BS = LRU_W // LRU_BLOCKS
CONV_W = 4
RG_C = 8.0
RET_H = 8
RET_DK = (D_MODEL // 2) // RET_H
RET_DV = RET_DK
ROPE_BASE = 10000.0
AB_IN = 2 * LRU_W + 2 * RET_H * RET_DK + 2 * RET_H * RET_DV
AB_OUT = LRU_W + RET_H * RET_DV
HG_DK = 128
HG_H = D_MODEL // HG_DK
HG_DV = D_MODEL // HG_H
C_IN = 2 * HG_H * HG_DK + 2 * HG_H * HG_DV
C_OUT = HG_H * HG_DV

kernel_name = 'hybrid_rglru_retention_hgrn2_macaron_step'


def rmsnorm(x, w):
    x32 = x.astype(jnp.float32)
    y = x32 * lax.rsqrt(jnp.mean(x32 * x32, axis=-1, keepdims=True) + EPS)
    return (y * w.astype(jnp.float32)).astype(x.dtype)


def head_rmsnorm(x):
    return x * lax.rsqrt(jnp.mean(x * x, axis=-1, keepdims=True) + EPS)


def swiglu(h, wg, wu, wd):
    return (jax.nn.silu(h @ wg) * (h @ wu)) @ wd


def rope(x, pos):
    half = x.shape[-1] // 2
    inv = ROPE_BASE ** (-jnp.arange(half, dtype=jnp.float32) / half)
    ang = pos[:, None] * inv[None, :]
    cos = jnp.cos(ang)[None, :, None, :]
    sin = jnp.sin(ang)[None, :, None, :]
    x1, x2 = x[..., :half], x[..., half:]
    return jnp.concatenate([x1 * cos - x2 * sin, x1 * sin + x2 * cos], axis=-1)


def to_chunks(a, L):
    B, T, H, D = a.shape
    return a.reshape(B, T // L, L, H, D).transpose(1, 0, 3, 2, 4)


def from_chunks(a):
    N, B, H, L, D = a.shape
    return a.transpose(1, 0, 3, 2, 4).reshape(B, N * L, H, D)


def rglru(x, buf, h0, conv_w, conv_b, wa, ba, wx, bx, lam):
    B, T, W = x.shape
    xp = jnp.concatenate([buf, x], axis=1)
    xc = conv_b + sum(xp[:, i:i + T] * conv_w[i] for i in range(CONV_W))
    new_buf = xp[:, T:]
    xb = xc.reshape(B, T, LRU_BLOCKS, LRU_BS)
    r = jax.nn.sigmoid(jnp.einsum('btnd,nde->btne', xb, wa).reshape(B, T, W) + ba)
    i_g = jax.nn.sigmoid(jnp.einsum('btnd,nde->btne', xb, wx).reshape(B, T, W) + bx)
    log_a = -RG_C * r * jax.nn.softplus(-lam)
    a = jnp.exp(log_a)
    b = jnp.sqrt(-jnp.expm1(2.0 * log_a)) * (i_g * xc)
    b = b.at[:, 0].add(a[:, 0] * h0)

    def combine(e1, e2):
        a1, b1 = e1
        a2, b2 = e2
        return (a1 * a2, a2 * b1 + b2)

    _, h = lax.associative_scan(combine, (a, b), axis=1)
    return h, new_buf, h[:, -1]


def retention(q, k, v, s0):
    T = q.shape[1]
    L = min(T, CHUNK)
    log_g = jnp.log1p(-jnp.exp2(-5.0 - jnp.arange(RET_H, dtype=jnp.float32)))
    idx = jnp.arange(L, dtype=jnp.float32)
    d_intra = jnp.exp(log_g[:, None, None] * jnp.abs(idx[:, None] - idx[None, :]))
    q_dec = jnp.exp(log_g[:, None] * (idx + 1.0))[..., None]
    k_dec = jnp.exp(log_g[:, None] * (L - 1.0 - idx))[..., None]
    c_dec = jnp.exp(log_g * L)[:, None, None]

    def step(s, blk):
        qb, kb, vb = blk
        att = jnp.einsum('bhld,bhmd->bhlm', qb, kb) * d_intra
        o = jnp.einsum('bhlm,bhmv->bhlv', att, vb) + jnp.einsum('bhld,bhdv->bhlv', qb * q_dec, s)
        s = c_dec * s + jnp.einsum('bhmd,bhmv->bhdv', kb * k_dec, vb)
        return s, o

    s, o = lax.scan(step, s0, (to_chunks(q, L), to_chunks(k, L), to_chunks(v, L)))
    return from_chunks(o), s


def hgrn2_scan(q, k, v, log_f, s0):
    T = q.shape[1]
    L = min(T, CHUNK)
    causal = jnp.tril(jnp.ones((L, L), dtype=bool))[:, :, None]

    def step(s, blk):
        qb, kb, vb, gb = blk
        c = jnp.cumsum(gb, axis=2)
        diff = c[:, :, :, None, :] - c[:, :, None, :, :]
        w = jnp.exp(jnp.where(causal, diff, -jnp.inf))
        att = jnp.einsum('bhtd,bhsd,bhtsd->bhts', qb, kb, w)
        o = jnp.einsum('bhts,bhsv->bhtv', att, vb) + jnp.einsum('bhtd,bhdv->bhtv', qb * jnp.exp(c), s)
        c_last = c[:, :, -1:, :]
        s = jnp.exp(c_last[:, :, 0, :, None]) * s + jnp.einsum('bhsd,bhsv->bhdv', kb * jnp.exp(c_last - c), vb)
        return s, o

    s, o = lax.scan(step, s0, (to_chunks(q, L), to_chunks(k, L), to_chunks(v, L), to_chunks(log_f, L)))
    return from_chunks(o), s


def mixer_ab(h, buf, h0, s0, pos, w_in, conv_w, conv_b, wa, ba, wx, bx, lam, ret_norm, w_out):
    B, T, _ = h.shape
    f32 = jnp.float32
    z = (h @ w_in).astype(f32)
    o1 = LRU_W
    o2 = o1 + LRU_W
    o3 = o2 + RET_H * RET_DK
    o4 = o3 + RET_H * RET_DK
    o5 = o4 + RET_H * RET_DV
    xa, ga, q, k, v, g = jnp.split(z, [o1, o2, o3, o4, o5], axis=-1)
    ya, new_buf, new_h = rglru(xa, buf.astype(f32), h0.astype(f32), conv_w, conv_b, wa, ba, wx, bx, lam)
    ya = ya * jax.nn.gelu(ga)
    q = rope(q.reshape(B, T, RET_H, RET_DK), pos) * (RET_DK ** -0.5)
    k = rope(k.reshape(B, T, RET_H, RET_DK), pos)
    v = v.reshape(B, T, RET_H, RET_DV)
    yr, new_s = retention(q, k, v, s0.astype(f32))
    yr = head_rmsnorm(yr).reshape(B, T, RET_H * RET_DV) * ret_norm * jax.nn.silu(g)
    out = jnp.concatenate([ya, yr], axis=-1).astype(h.dtype) @ w_out
    return out, new_buf, new_h, new_s


def mixer_c(h, s0, lb, w_in, norm_w, w_out):
    B, T, _ = h.shape
    f32 = jnp.float32
    z = (h @ w_in).astype(f32)
    o1 = HG_H * HG_DK
    o2 = 2 * HG_H * HG_DK
    o3 = o2 + HG_H * HG_DV
    q, fz, iv, g = jnp.split(z, [o1, o2, o3], axis=-1)
    f = lb + (1.0 - lb) * jax.nn.sigmoid(fz)
    q = jax.nn.silu(q).reshape(B, T, HG_H, HG_DK) * (HG_DK ** -0.5)
    k = (1.0 - f).reshape(B, T, HG_H, HG_DK)
    log_f = jnp.log(f).reshape(B, T, HG_H, HG_DK)
    v = iv.reshape(B, T, HG_H, HG_DV)
    o, new_s = hgrn2_scan(q, k, v, log_f, s0.astype(f32))
    o = head_rmsnorm(o).reshape(B, T, C_OUT) * norm_w * jax.nn.silu(g)
    return o.astype(h.dtype) @ w_out, new_s


def trunk(x, conv_state, lru_state, ret_state, hg_state, pos0,
          ffn1_norm, ffn1_wg, ffn1_wu, ffn1_wd, mix_norm, ffn2_norm, ffn2_wg, ffn2_wu, ffn2_wd,
          final_norm, ab_w_in, ab_conv_w, ab_conv_b, ab_gate_a_w, ab_gate_a_b, ab_gate_x_w,
          ab_gate_x_b, ab_lru_lambda, ab_ret_norm, ab_w_out, c_w_in, c_lb_logits, c_norm, c_w_out):
    T = x.shape[1]
    pos = pos0 + jnp.arange(T, dtype=jnp.float32)
    p = jax.nn.softmax(c_lb_logits.astype(jnp.float32), axis=0)
    lb_all = jnp.cumsum(p, axis=0) - p[0]
    new_conv, new_lru, new_ret, new_hg = [], [], [], []
    for l in range(DEPTH):
        j = l // 2
        x = x + 0.5 * swiglu(rmsnorm(x, ffn1_norm[l]), ffn1_wg[l], ffn1_wu[l], ffn1_wd[l])
        h = rmsnorm(x, mix_norm[l])
        if l % 2 == 0:
            out, nb, nh, ns = mixer_ab(h, conv_state[j], lru_state[j], ret_state[j], pos,
                                       ab_w_in[j], ab_conv_w[j], ab_conv_b[j], ab_gate_a_w[j],
                                       ab_gate_a_b[j], ab_gate_x_w[j], ab_gate_x_b[j],
                                       ab_lru_lambda[j], ab_ret_norm[j], ab_w_out[j])
            new_conv.append(nb)
            new_lru.append(nh)
            new_ret.append(ns)
        else:
            out, ns = mixer_c(h, hg_state[j], lb_all[l], c_w_in[j], c_norm[j], c_w_out[j])
            new_hg.append(ns)
        x = x + out
        x = x + 0.5 * swiglu(rmsnorm(x, ffn2_norm[l]), ffn2_wg[l], ffn2_wu[l], ffn2_wd[l])
    y = rmsnorm(x, final_norm)
    return y, jnp.stack(new_conv), jnp.stack(new_lru), jnp.stack(new_ret), jnp.stack(new_hg)


def setup_inputs(seed: int = 0) -> dict:
    key = jax.random.key(seed)
    ks = list(jax.random.split(key, 40))
    f32 = jnp.float32

    def nrm(shape, scale):
        return jax.random.normal(ks.pop(), shape, f32) * scale

    def gain(shape):
        return 1.0 + nrm(shape, 0.02)

    u = jax.random.uniform(ks.pop(), (N_AB, LRU_W), f32, 0.9, 0.999)
    sig = u ** (1.0 / RG_C)
    lam = jnp.log(sig) - jnp.log1p(-sig)
    return {
        'x_prompt': nrm((BATCH, SEQ, D_MODEL), 1.0),
        'x_sample': nrm((DEC_BATCH, DEC_SEQ, D_MODEL), 1.0),
        'state_conv': nrm((N_AB, DEC_BATCH, CONV_W - 1, LRU_W), 1.0),
        'state_lru': nrm((N_AB, DEC_BATCH, LRU_W), 0.5),
        'state_ret': nrm((N_AB, DEC_BATCH, RET_H, RET_DK, RET_DV), 0.1),
        'state_hgrn': nrm((N_C, DEC_BATCH, HG_H, HG_DK, HG_DV), 0.5),
        'ffn1_norm': gain((DEPTH, D_MODEL)),
        'ffn1_wg': nrm((DEPTH, D_MODEL, D_FF), D_MODEL ** -0.5),
        'ffn1_wu': nrm((DEPTH, D_MODEL, D_FF), D_MODEL ** -0.5),
        'ffn1_wd': nrm((DEPTH, D_FF, D_MODEL), D_FF ** -0.5),
        'mix_norm': gain((DEPTH, D_MODEL)),
        'ffn2_norm': gain((DEPTH, D_MODEL)),
        'ffn2_wg': nrm((DEPTH, D_MODEL, D_FF), D_MODEL ** -0.5),
        'ffn2_wu': nrm((DEPTH, D_MODEL, D_FF), D_MODEL ** -0.5),
        'ffn2_wd': nrm((DEPTH, D_FF, D_MODEL), D_FF ** -0.5),
        'final_norm': gain((D_MODEL,)),
        'ab_w_in': nrm((N_AB, D_MODEL, AB_IN), D_MODEL ** -0.5),
        'ab_conv_w': nrm((N_AB, CONV_W, LRU_W), CONV_W ** -0.5),
        'ab_conv_b': nrm((N_AB, LRU_W), 0.01),
        'ab_gate_a_w': nrm((N_AB, LRU_BLOCKS, LRU_BS, LRU_BS), LRU_BS ** -0.5),
        'ab_gate_a_b': nrm((N_AB, LRU_W), 0.01),
        'ab_gate_x_w': nrm((N_AB, LRU_BLOCKS, LRU_BS, LRU_BS), LRU_BS ** -0.5),
        'ab_gate_x_b': nrm((N_AB, LRU_W), 0.01),
        'ab_lru_lambda': lam,
        'ab_ret_norm': gain((N_AB, RET_H * RET_DV)),
        'ab_w_out': nrm((N_AB, AB_OUT, D_MODEL), AB_OUT ** -0.5),
        'c_w_in': nrm((N_C, D_MODEL, C_IN), D_MODEL ** -0.5),
        'c_lb_logits': nrm((DEPTH, HG_H * HG_DK), 0.1),
        'c_norm': gain((N_C, C_OUT)),
        'c_w_out': nrm((N_C, C_OUT, D_MODEL), C_OUT ** -0.5),
    }


def reference(x_prompt, x_sample, state_conv, state_lru, state_ret, state_hgrn,
              ffn1_norm, ffn1_wg, ffn1_wu, ffn1_wd, mix_norm, ffn2_norm, ffn2_wg, ffn2_wu, ffn2_wd,
              final_norm, ab_w_in, ab_conv_w, ab_conv_b, ab_gate_a_w, ab_gate_a_b, ab_gate_x_w,
              ab_gate_x_b, ab_lru_lambda, ab_ret_norm, ab_w_out, c_w_in, c_lb_logits, c_norm, c_w_out):
    weights = (ffn1_norm, ffn1_wg, ffn1_wu, ffn1_wd, mix_norm, ffn2_norm, ffn2_wg, ffn2_wu, ffn2_wd,
               final_norm, ab_w_in, ab_conv_w, ab_conv_b, ab_gate_a_w, ab_gate_a_b, ab_gate_x_w,
               ab_gate_x_b, ab_lru_lambda, ab_ret_norm, ab_w_out, c_w_in, c_lb_logits, c_norm, c_w_out)
    B = x_prompt.shape[0]
    f32 = jnp.float32
    zc = jnp.zeros((N_AB, B, CONV_W - 1, LRU_W), f32)
    zl = jnp.zeros((N_AB, B, LRU_W), f32)
    zr = jnp.zeros((N_AB, B, RET_H, RET_DK, RET_DV), f32)
    zh = jnp.zeros((N_C, B, HG_H, HG_DK, HG_DV), f32)
    y_prompt, conv_p, lru_p, ret_p, hg_p = trunk(x_prompt, zc, zl, zr, zh, 0, *weights)
    y_sample, conv_s, lru_s, ret_s, hg_s = trunk(x_sample, state_conv, state_lru, state_ret,
                                                 state_hgrn, PAST_LEN, *weights)
    return (y_prompt, y_sample, conv_p, lru_p, ret_p, hg_p, conv_s, lru_s, ret_s, hg_s)
```

```python
import functools

import jax
import jax.numpy as jnp
from jax import lax
from jax.experimental import pallas as pl
from jax.experimental.pallas import tpu as pltpu

F32 = jnp.float32
BF16 = jnp.bfloat16

D_MODEL = 2048
CHUNK = 64
D_FF = 5632
EPS = 1e-6
PAST_LEN = 2048
LRU_W = 1024
LRU_BLOCKS = 8
LRU_BS = 128
CONV_W = 4
RG_C = 8.0
RET_H = 8
RET_DK = 128
ROPE_BASE = 10000.0
HG_H = 16
HG_DK = 128

SUBLANES = 8
LANES = 128
VMEM_LIMIT = 56 * 1024 * 1024

NT_DIMS = (((1,), (1,)), ((), ()))
TN_DIMS = (((0,), (0,)), ((), ()))


def _sigmoid(x):
    return 1.0 / (1.0 + jnp.exp(-x))


def _silu(x):
    return x * _sigmoid(x)


def _gelu_tanh(x):
    return 0.5 * x * (1.0 + jnp.tanh(0.7978845608028654 * (x + 0.044715 * (x * x * x))))


def _rms_rows(x, w):
    ms = jnp.mean(x * x, axis=-1, keepdims=True)
    return x * lax.rsqrt(ms + EPS) * w


def _params(n_axes):
    return pltpu.CompilerParams(dimension_semantics=("arbitrary",) * n_axes,
                                vmem_limit_bytes=VMEM_LIMIT)


ROW_STEP = 64


def _norm_rows_to(x_ref, w_ref, h_ref, tm):
    def step(i, _):
        rows = pl.ds(pl.multiple_of(i * ROW_STEP, ROW_STEP), ROW_STEP)
        h_ref[rows, :] = _rms_rows(x_ref[rows, :], w_ref[...]).astype(h_ref.dtype)
        return 0
    lax.fori_loop(0, tm // ROW_STEP, step, 0)


def _ffn_body(x_ref, nw_ref, wg_ref, wu_ref, wd_ref, fnw_ref, o_ref, h_ref, *, tm, nk, final):
    k = pl.program_id(1)

    @pl.when(k == 0)
    def _():
        _norm_rows_to(x_ref, nw_ref, h_ref, tm)
        o_ref[...] = jnp.zeros_like(o_ref)

    h = h_ref[...]
    g = jnp.dot(h, wg_ref[...], preferred_element_type=F32)
    u = jnp.dot(h, wu_ref[...], preferred_element_type=F32)
    a = (_silu(g) * u).astype(BF16)
    o_ref[...] += jnp.dot(a, wd_ref[...], preferred_element_type=F32)

    @pl.when(k == nk - 1)
    def _():
        def step(i, _):
            rows = pl.ds(pl.multiple_of(i * ROW_STEP, ROW_STEP), ROW_STEP)
            y = x_ref[rows, :] + 0.5 * o_ref[rows, :]
            if final:
                y = _rms_rows(y, fnw_ref[...])
            o_ref[rows, :] = y
            return 0
        lax.fori_loop(0, tm // ROW_STEP, step, 0)


def _ffn(x, nw, wg, wu, wd, fnw, *, tm, tf, final):
    m = x.shape[0]
    nk = D_FF // tf
    body = functools.partial(_ffn_body, tm=tm, nk=nk, final=final)
    return pl.pallas_call(
        body,
        grid=(m // tm, nk),
        in_specs=[
            pl.BlockSpec((tm, D_MODEL), lambda i, k: (i, 0)),
            pl.BlockSpec((1, D_MODEL), lambda i, k: (0, 0)),
            pl.BlockSpec((D_MODEL, tf), lambda i, k: (0, k)),
            pl.BlockSpec((D_MODEL, tf), lambda i, k: (0, k)),
            pl.BlockSpec((tf, D_MODEL), lambda i, k: (k, 0)),
            pl.BlockSpec((1, D_MODEL), lambda i, k: (0, 0)),
        ],
        out_specs=pl.BlockSpec((tm, D_MODEL), lambda i, k: (i, 0)),
        out_shape=jax.ShapeDtypeStruct((m, D_MODEL), F32),
        scratch_shapes=[pltpu.VMEM((tm, D_MODEL), BF16)],
        compiler_params=_params(2),
        name="ffn",
    )(x, nw, wg, wu, wd, fnw)


def _normproj_body(x_ref, nw_ref, w_ref, o_ref, h_ref, *, tm):
    @pl.when(pl.program_id(1) == 0)
    def _():
        _norm_rows_to(x_ref, nw_ref, h_ref, tm)

    o_ref[...] = jnp.dot(h_ref[...], w_ref[...], preferred_element_type=F32)


def _normproj(x, nw, w, *, tm, tn):
    m = x.shape[0]
    n = w.shape[1]
    return pl.pallas_call(
        functools.partial(_normproj_body, tm=tm),
        grid=(m // tm, n // tn),
        in_specs=[
            pl.BlockSpec((tm, D_MODEL), lambda i, j: (i, 0)),
            pl.BlockSpec((1, D_MODEL), lambda i, j: (0, 0)),
            pl.BlockSpec((D_MODEL, tn), lambda i, j: (0, j)),
        ],
        out_specs=pl.BlockSpec((tm, tn), lambda i, j: (i, j)),
        out_shape=jax.ShapeDtypeStruct((m, n), F32),
        scratch_shapes=[pltpu.VMEM((tm, D_MODEL), BF16)],
        compiler_params=_params(2),
        name="normproj",
    )(x, nw, w)


def _projout_body(*refs, n_in):
    ys = refs[:n_in]
    ws = refs[n_in:2 * n_in]
    res_ref = refs[2 * n_in]
    o_ref = refs[2 * n_in + 1]
    acc = res_ref[...]
    for y_ref, w_ref in zip(ys, ws):
        acc = acc + jnp.dot(y_ref[...], w_ref[...], preferred_element_type=F32)
    o_ref[...] = acc


def _projout(ys, w, res, *, tm):
    m = res.shape[0]
    n_in = len(ys)
    kw = ys[0].shape[1]
    in_specs = [pl.BlockSpec((tm, kw), lambda i: (i, 0)) for _ in ys]
    in_specs += [pl.BlockSpec((kw, D_MODEL), functools.partial(lambda i, j: (j, 0), j=j))
                 for j in range(n_in)]
    in_specs += [pl.BlockSpec((tm, D_MODEL), lambda i: (i, 0))]
    return pl.pallas_call(
        functools.partial(_projout_body, n_in=n_in),
        grid=(m // tm,),
        in_specs=in_specs,
        out_specs=pl.BlockSpec((tm, D_MODEL), lambda i: (i, 0)),
        out_shape=jax.ShapeDtypeStruct((m, D_MODEL), F32),
        compiler_params=_params(1),
        name="projout",
    )(*ys, *([w] * n_in), res)


def _rglru_body(xa_ref, ga_ref, buf0_ref, h0_ref, cw_ref, cb_ref, wa_ref, ba_ref, wx_ref, bx_ref,
                lam_ref, ya_ref, nbuf_ref, nh_ref, ext_ref, a_ref, b_ref, hc_ref, *, tb, nt):
    t = pl.program_id(1)
    halo = SUBLANES

    @pl.when(t == 0)
    def _():
        ext_ref[0:halo, :] = jnp.zeros((halo, LRU_W), F32)
        ext_ref[halo - (CONV_W - 1):halo, :] = buf0_ref[0]
        hc_ref[...] = h0_ref[0]

    @pl.when(t > 0)
    def _():
        ext_ref[0:halo, :] = ext_ref[tb:tb + halo, :]

    x = xa_ref[...]
    ext_ref[halo:halo + tb, :] = x
    xc = ext_ref[halo - 3:halo - 3 + tb, :] * cw_ref[0:1, :]
    xc = xc + ext_ref[halo - 2:halo - 2 + tb, :] * cw_ref[1:2, :]
    xc = xc + ext_ref[halo - 1:halo - 1 + tb, :] * cw_ref[2:3, :]
    xc = xc + x * cw_ref[3:4, :]
    xc = cb_ref[...] + xc

    for n in range(LRU_BLOCKS):
        sl = slice(n * LRU_BS, (n + 1) * LRU_BS)
        xcn = xc[:, sl]
        xb = xcn.astype(BF16)
        r = _sigmoid(jnp.dot(xb, wa_ref[n], preferred_element_type=F32) + ba_ref[:, sl])
        ig = _sigmoid(jnp.dot(xb, wx_ref[n], preferred_element_type=F32) + bx_ref[:, sl])
        nlam = -lam_ref[:, sl]
        softplus = jnp.maximum(nlam, 0.0) + jnp.log(1.0 + jnp.exp(-jnp.abs(nlam)))
        log_a = (-RG_C) * r * softplus
        a = jnp.exp(log_a)
        mult = jnp.sqrt(-jnp.tanh(log_a) * (a * a + 1.0))
        a_ref[:, sl] = a
        b_ref[:, sl] = mult * (ig * xcn)

    rowi = lax.broadcasted_iota(jnp.int32, (SUBLANES, LRU_W), 0)

    def group(gi, carry):
        rows = pl.ds(pl.multiple_of(gi * SUBLANES, SUBLANES), SUBLANES)
        av = a_ref[rows, :]
        bv = b_ref[rows, :]
        for s in (1, 2, 4):
            m = rowi >= s
            ash = pltpu.roll(av, s, 0)
            bsh = pltpu.roll(bv, s, 0)
            bv = jnp.where(m, av * bsh + bv, bv)
            av = jnp.where(m, av * ash, av)
        hrows = av * carry + bv
        b_ref[rows, :] = hrows
        return hrows[SUBLANES - 1:SUBLANES, :]

    carry = lax.fori_loop(0, tb // SUBLANES, group, hc_ref[...])
    hc_ref[...] = carry

    ya_ref[...] = (b_ref[...] * _gelu_tanh(ga_ref[...])).astype(ya_ref.dtype)

    @pl.when(t == nt - 1)
    def _():
        nbuf_ref[0] = ext_ref[halo + tb - (CONV_W - 1):halo + tb, :]
        nh_ref[0] = carry


def _rglru(z, buf0, h0, cw, cb, wa, ba, wx, bx, lam, *, nb, t_len, tb):
    nt = t_len // tb
    row1 = lambda b, t: (0, 0)
    return pl.pallas_call(
        functools.partial(_rglru_body, tb=tb, nt=nt),
        grid=(nb, nt),
        in_specs=[
            pl.BlockSpec((tb, LRU_W), lambda b, t: (b * nt + t, 0)),
            pl.BlockSpec((tb, LRU_W), lambda b, t: (b * nt + t, 1)),
            pl.BlockSpec((1, CONV_W - 1, LRU_W), lambda b, t: (b, 0, 0)),
            pl.BlockSpec((1, 1, LRU_W), lambda b, t: (b, 0, 0)),
            pl.BlockSpec((CONV_W, LRU_W), row1),
            pl.BlockSpec((1, LRU_W), row1),
            pl.BlockSpec((LRU_BLOCKS, LRU_BS, LRU_BS), lambda b, t: (0, 0, 0)),
            pl.BlockSpec((1, LRU_W), row1),
            pl.BlockSpec((LRU_BLOCKS, LRU_BS, LRU_BS), lambda b, t: (0, 0, 0)),
            pl.BlockSpec((1, LRU_W), row1),
            pl.BlockSpec((1, LRU_W), row1),
        ],
        out_specs=[
            pl.BlockSpec((tb, LRU_W), lambda b, t: (b * nt + t, 0)),
            pl.BlockSpec((1, CONV_W - 1, LRU_W), lambda b, t: (b, 0, 0)),
            pl.BlockSpec((1, 1, LRU_W), lambda b, t: (b, 0, 0)),
        ],
        out_shape=[
            jax.ShapeDtypeStruct((nb * t_len, LRU_W), BF16),
            jax.ShapeDtypeStruct((nb, CONV_W - 1, LRU_W), F32),
            jax.ShapeDtypeStruct((nb, 1, LRU_W), F32),
        ],
        scratch_shapes=[
            pltpu.VMEM((tb + SUBLANES, LRU_W), F32),
            pltpu.VMEM((tb, LRU_W), F32),
            pltpu.VMEM((tb, LRU_W), F32),
            pltpu.VMEM((1, LRU_W), F32),
        ],
        compiler_params=_params(2),
        name="rglru",
    )(z, z, buf0, h0, cw, cb, wa, ba, wx, bx, lam)


def _retention_body(q_ref, k_ref, v_ref, g_ref, cos_ref, sin_ref, s0_ref, lg_ref, nw_ref,
                    y_ref, so_ref, s_ref, *, lb, nsub, nt, chunk):
    t = pl.program_id(2)

    @pl.when(t == 0)
    def _():
        s_ref[...] = s0_ref[0, 0]

    lg = lg_ref[0][:, 0:1]
    ti = lax.broadcasted_iota(jnp.int32, (lb, lb), 0)
    si = lax.broadcasted_iota(jnp.int32, (lb, lb), 1)
    dt = (ti - si).astype(F32)
    tc = ti & (-chunk)
    sc = si & (-chunk)
    dmask = jnp.where(tc == sc, jnp.exp(lg * jnp.abs(dt)),
                      jnp.where(sc < tc, jnp.exp(lg * jnp.maximum(dt, 0.0)), 0.0))
    rowf = lax.broadcasted_iota(jnp.int32, (lb, RET_DK), 0).astype(F32)
    qdec = jnp.exp(lg * (rowf + 1.0))
    kdec = jnp.exp(lg * ((lb - 1.0) - rowf))
    cdec = jnp.exp(lg * float(lb))
    scale = RET_DK ** -0.5
    nw = nw_ref[...]

    def sub(j, _):
        rows = pl.ds(pl.multiple_of(j * lb, lb), lb)
        q = q_ref[rows, :]
        k = k_ref[rows, :]
        v = v_ref[rows, :].astype(BF16)
        cs = cos_ref[rows, :]
        sn = sin_ref[rows, :]
        qr = (q * cs + pltpu.roll(q, RET_DK // 2, 1) * sn) * scale
        kr = k * cs + pltpu.roll(k, RET_DK // 2, 1) * sn
        att = lax.dot_general(qr.astype(BF16), kr.astype(BF16), NT_DIMS,
                              preferred_element_type=F32) * dmask
        s = s_ref[...]
        o = jnp.dot(att.astype(BF16), v, preferred_element_type=F32)
        o = o + jnp.dot((qr * qdec).astype(BF16), s.astype(BF16), preferred_element_type=F32)
        s_ref[...] = cdec * s + lax.dot_general((kr * kdec).astype(BF16), v, TN_DIMS,
                                                preferred_element_type=F32)
        yn = o * lax.rsqrt(jnp.mean(o * o, axis=-1, keepdims=True) + EPS)
        y_ref[rows, :] = (yn * nw * _silu(g_ref[rows, :])).astype(y_ref.dtype)
        return 0

    lax.fori_loop(0, nsub, sub, 0)

    @pl.when(t == nt - 1)
    def _():
        so_ref[0, 0] = s_ref[...]


def _retention(z, cosf, sinf, s0, lgt, nw, *, nb, t_len, tc, lb, chunk, shared_pos):
    nt = t_len // tc
    col0 = 2 * LRU_W // RET_DK
    pos_map = (lambda b, h, t: (0, 0)) if shared_pos else (lambda b, h, t: (b * nt + t, 0))

    def zspec(off):
        return pl.BlockSpec((tc, RET_DK), lambda b, h, t: (b * nt + t, col0 + off * RET_H + h))

    return pl.pallas_call(
        functools.partial(_retention_body, lb=lb, nsub=tc // lb, nt=nt, chunk=chunk),
        grid=(nb, RET_H, nt),
        in_specs=[
            zspec(0), zspec(1), zspec(2), zspec(3),
            pl.BlockSpec((tc, RET_DK), pos_map),
            pl.BlockSpec((tc, RET_DK), pos_map),
            pl.BlockSpec((1, 1, RET_DK, RET_DK), lambda b, h, t: (b, h, 0, 0)),
            pl.BlockSpec((1, 1, RET_DK), lambda b, h, t: (h, 0, 0)),
            pl.BlockSpec((1, RET_DK), lambda b, h, t: (0, h)),
        ],
        out_specs=[
            pl.BlockSpec((tc, RET_DK), lambda b, h, t: (b * nt + t, h)),
            pl.BlockSpec((1, 1, RET_DK, RET_DK), lambda b, h, t: (b, h, 0, 0)),
        ],
        out_shape=[
            jax.ShapeDtypeStruct((nb * t_len, RET_H * RET_DK), BF16),
            jax.ShapeDtypeStruct((nb, RET_H, RET_DK, RET_DK), F32),
        ],
        scratch_shapes=[pltpu.VMEM((RET_DK, RET_DK), F32)],
        compiler_params=_params(3),
        name="retention",
    )(z, z, z, z, cosf, sinf, s0, lgt, nw)


def _hgrn_body(q_ref, f_ref, v_ref, g_ref, lbl_ref, nw_ref, s0_ref, y_ref, so_ref, st_ref,
               *, chunk, nsub, nt, layer):
    t = pl.program_id(2)
    L = chunk

    @pl.when(t == 0)
    def _():
        st_ref[...] = s0_ref[0, 0].T

    lgt = lbl_ref[...]
    mx = jnp.max(lgt, axis=0, keepdims=True)
    ex = jnp.exp(lgt - mx)
    p = ex / jnp.sum(ex, axis=0, keepdims=True)
    lbv = jnp.sum(p[0:layer + 1, :], axis=0, keepdims=True) - p[0:1, :]

    rowi = lax.broadcasted_iota(jnp.int32, (L, HG_DK), 0)
    ti = lax.broadcasted_iota(jnp.int32, (L, L), 0)
    si = lax.broadcasted_iota(jnp.int32, (L, L), 1)
    txs = ti ^ si
    nw = nw_ref[...]
    scale = HG_DK ** -0.5

    def nt_dot(a, b):
        return lax.dot_general(a.astype(BF16), b.astype(BF16), NT_DIMS, preferred_element_type=F32)

    def sub(j, _):
        rows = pl.ds(pl.multiple_of(j * L, L), L)
        f = lbv + (1.0 - lbv) * _sigmoid(f_ref[rows, :])
        qh = _silu(q_ref[rows, :]) * scale
        kk = 1.0 - f
        v = v_ref[rows, :].astype(BF16)
        c = jnp.log(f)
        s = 1
        while s < L:
            c = c + jnp.where(rowi >= s, pltpu.roll(c, s, 0), 0.0)
            s *= 2
        clast = c[L - 1:L, :]

        att = jnp.zeros((L, L), F32)
        blk = L
        while blk >= SUBLANES:
            half = blk // 2
            nblk = L // blk
            r = jnp.concatenate(
                [jnp.broadcast_to(c[i * blk + half - 1:i * blk + half, :], (blk, HG_DK))
                 for i in range(nblk)], axis=0)
            e = jnp.exp(-jnp.abs(c - r))
            upper = (rowi & (blk - 1)) >= half
            a_l = nt_dot(jnp.where(upper, qh * e, 0.0), jnp.where(upper, 0.0, kk * e))
            att = att + (a_l if nblk == 1 else jnp.where(txs < blk, a_l, 0.0))
            blk = half
        fprev = pltpu.roll(f, 1, 0)
        fnext = pltpu.roll(f, L - 1, 0)
        qf = qh * f
        i4 = rowi & 3
        q4 = jnp.where(i4 == 2, qf, jnp.where(i4 == 3, qf * fprev, 0.0))
        k4 = jnp.where(i4 == 0, kk * fnext, jnp.where(i4 == 1, kk, 0.0))
        att = att + jnp.where(txs < 4, nt_dot(q4, k4), 0.0)
        odd = (rowi & 1) == 1
        att = att + jnp.where(txs < 2, nt_dot(jnp.where(odd, qf, 0.0), jnp.where(odd, 0.0, kk)), 0.0)
        att = jnp.where(ti == si, jnp.sum(qh * kk, axis=-1, keepdims=True), att)

        st = st_ref[...]
        o = jnp.dot(att.astype(BF16), v, preferred_element_type=F32)
        o = o + nt_dot(qh * jnp.exp(c), st)
        kd = (kk * jnp.exp(clast - c)).astype(BF16)
        st_ref[...] = jnp.exp(clast) * st + lax.dot_general(v, kd, TN_DIMS, preferred_element_type=F32)
        yn = o * lax.rsqrt(jnp.mean(o * o, axis=-1, keepdims=True) + EPS)
        y_ref[rows, :] = (yn * nw * _silu(g_ref[rows, :])).astype(y_ref.dtype)
        return 0

    lax.fori_loop(0, nsub, sub, 0)

    @pl.when(t == nt - 1)
    def _():
        so_ref[0, 0] = st_ref[...].T


def _hgrn(z, logits, nw, s0, *, nb, t_len, tc, chunk, layer):
    nt = t_len // tc

    def zspec(off):
        return pl.BlockSpec((tc, HG_DK), lambda b, h, t: (b * nt + t, off * HG_H + h))

    return pl.pallas_call(
        functools.partial(_hgrn_body, chunk=chunk, nsub=tc // chunk, nt=nt, layer=layer),
        grid=(nb, HG_H, nt),
        in_specs=[
            zspec(0), zspec(1), zspec(2), zspec(3),
            pl.BlockSpec((logits.shape[0], HG_DK), lambda b, h, t: (0, h)),
            pl.BlockSpec((1, HG_DK), lambda b, h, t: (0, h)),
            pl.BlockSpec((1, 1, HG_DK, HG_DK), lambda b, h, t: (b, h, 0, 0)),
        ],
        out_specs=[
            pl.BlockSpec((tc, HG_DK), lambda b, h, t: (b * nt + t, h)),
            pl.BlockSpec((1, 1, HG_DK, HG_DK), lambda b, h, t: (b, h, 0, 0)),
        ],
        out_shape=[
            jax.ShapeDtypeStruct((nb * t_len, HG_H * HG_DK), BF16),
            jax.ShapeDtypeStruct((nb, HG_H, HG_DK, HG_DK), F32),
        ],
        scratch_shapes=[pltpu.VMEM((HG_DK, HG_DK), F32)],
        compiler_params=_params(3),
        name="hgrn",
    )(z, z, z, z, logits, nw, s0)


def _rope_tables(pos):
    half = RET_DK // 2
    inv = ROPE_BASE ** (-jnp.arange(half, dtype=F32) / half)
    ang = pos[:, None] * inv[None, :]
    cos = jnp.cos(ang)
    sin = jnp.sin(ang)
    return jnp.concatenate([cos, cos], axis=-1), jnp.concatenate([-sin, sin], axis=-1)


def _trunk(x, conv0, lru0, ret0, hg0, pos0, w, *, nb, t_len, tm, tb, ret_tc, ret_lb, hg_tc):
    chunk = min(t_len, CHUNK)
    cosf, sinf = _rope_tables(pos0 + jnp.arange(t_len, dtype=F32))
    log_g = jnp.log1p(-jnp.exp2(-5.0 - jnp.arange(RET_H, dtype=F32)))
    lgt = jnp.broadcast_to(log_g[:, None, None], (RET_H, 1, RET_DK))
    row = lambda a: a.reshape(1, -1)

    x = _ffn(x, row(w["ffn1_norm"][0]), w["ffn1_wg"][0], w["ffn1_wu"][0], w["ffn1_wd"][0],
             row(w["final_norm"]), tm=tm, tf=512, final=False)
    z = _normproj(x, row(w["mix_norm"][0]), w["ab_w_in"][0], tm=tm, tn=512)
    ya, nbuf, nh = _rglru(z, conv0, lru0.reshape(nb, 1, LRU_W), w["ab_conv_w"][0], row(w["ab_conv_b"][0]),
                          w["ab_gate_a_w"][0], row(w["ab_gate_a_b"][0]), w["ab_gate_x_w"][0],
                          row(w["ab_gate_x_b"][0]), row(w["ab_lru_lambda"][0]),
                          nb=nb, t_len=t_len, tb=tb)
    yr, ns = _retention(z, cosf, sinf, ret0, lgt, row(w["ab_ret_norm"][0]), nb=nb, t_len=t_len,
                        tc=ret_tc, lb=ret_lb, chunk=chunk, shared_pos=nb > 1)
    x = _projout([ya, yr], w["ab_w_out"][0], x, tm=min(tm, 512))
    x = _ffn(x, row(w["ffn2_norm"][0]), w["ffn2_wg"][0], w["ffn2_wu"][0], w["ffn2_wd"][0],
             row(w["final_norm"]), tm=tm, tf=512, final=False)

    x = _ffn(x, row(w["ffn1_norm"][1]), w["ffn1_wg"][1], w["ffn1_wu"][1], w["ffn1_wd"][1],
             row(w["final_norm"]), tm=tm, tf=512, final=False)
    z = _normproj(x, row(w["mix_norm"][1]), w["c_w_in"][0], tm=tm, tn=512)
    yc, nhg = _hgrn(z, w["c_lb_logits"], row(w["c_norm"][0]), hg0, nb=nb, t_len=t_len, tc=hg_tc,
                    chunk=chunk, layer=1)
    x = _projout([yc], w["c_w_out"][0], x, tm=min(tm, 512))
    y = _ffn(x, row(w["ffn2_norm"][1]), w["ffn2_wg"][1], w["ffn2_wu"][1], w["ffn2_wd"][1],
             row(w["final_norm"]), tm=tm, tf=512, final=True)
    return (y.reshape(nb, t_len, D_MODEL), nbuf[None], nh.reshape(1, nb, LRU_W), ns[None], nhg[None])


_MATMUL_WEIGHTS = ("ffn1_wg", "ffn1_wu", "ffn1_wd", "ffn2_wg", "ffn2_wu", "ffn2_wd",
                   "ab_w_in", "ab_gate_a_w", "ab_gate_x_w", "ab_w_out", "c_w_in", "c_w_out")


def kernel(x_prompt, x_sample, state_conv, state_lru, state_ret, state_hgrn, ffn1_norm, ffn1_wg, ffn1_wu, ffn1_wd, mix_norm, ffn2_norm, ffn2_wg, ffn2_wu, ffn2_wd, final_norm, ab_w_in, ab_conv_w, ab_conv_b, ab_gate_a_w, ab_gate_a_b, ab_gate_x_w, ab_gate_x_b, ab_lru_lambda, ab_ret_norm, ab_w_out, c_w_in, c_lb_logits, c_norm, c_w_out):
    w = dict(ffn1_norm=ffn1_norm, ffn1_wg=ffn1_wg, ffn1_wu=ffn1_wu, ffn1_wd=ffn1_wd, mix_norm=mix_norm,
             ffn2_norm=ffn2_norm, ffn2_wg=ffn2_wg, ffn2_wu=ffn2_wu, ffn2_wd=ffn2_wd,
             final_norm=final_norm, ab_w_in=ab_w_in, ab_conv_w=ab_conv_w, ab_conv_b=ab_conv_b,
             ab_gate_a_w=ab_gate_a_w, ab_gate_a_b=ab_gate_a_b, ab_gate_x_w=ab_gate_x_w,
             ab_gate_x_b=ab_gate_x_b, ab_lru_lambda=ab_lru_lambda, ab_ret_norm=ab_ret_norm,
             ab_w_out=ab_w_out, c_w_in=c_w_in, c_lb_logits=c_lb_logits, c_norm=c_norm, c_w_out=c_w_out)
    for name in _MATMUL_WEIGHTS:
        w[name] = w[name].astype(BF16)

    nbp, tp, _ = x_prompt.shape
    nbs, ts, _ = x_sample.shape
    zc = jnp.zeros((nbp, CONV_W - 1, LRU_W), F32)
    zl = jnp.zeros((nbp, LRU_W), F32)
    zr = jnp.zeros((nbp, RET_H, RET_DK, RET_DK), F32)
    zh = jnp.zeros((nbp, HG_H, HG_DK, HG_DK), F32)
    yp, cp, lp, rp, hp = _trunk(x_prompt.reshape(nbp * tp, D_MODEL), zc, zl, zr, zh, 0.0, w,
                                nb=nbp, t_len=tp, tm=1024, tb=256, ret_tc=1024, ret_lb=256, hg_tc=1024)
    ys, cs, ls, rs, hs = _trunk(x_sample.reshape(nbs * ts, D_MODEL), state_conv[0], state_lru[0],
                                state_ret[0], state_hgrn[0], float(PAST_LEN), w,
                                nb=nbs, t_len=ts, tm=nbs * ts, tb=ts, ret_tc=ts, ret_lb=ts, hg_tc=ts)
    return (yp, ys, cp, lp, rp, hp, cs, ls, rs, hs)
```

```python
import functools

import jax
import jax.numpy as jnp
from jax import lax
from jax.experimental import pallas as pl
from jax.experimental.pallas import tpu as pltpu

F32 = jnp.float32
BF16 = jnp.bfloat16

D_MODEL = 2048
CHUNK = 64
D_FF = 5632
EPS = 1e-6
PAST_LEN = 2048
LRU_W = 1024
LRU_BLOCKS = 8
LRU_BS = 128
CONV_W = 4
RG_C = 8.0
RET_H = 8
RET_DK = 128
ROPE_BASE = 10000.0
HG_H = 16
HG_DK = 128

SUBLANES = 8
LANES = 128
VMEM_LIMIT = 56 * 1024 * 1024

NT_DIMS = (((1,), (1,)), ((), ()))
TN_DIMS = (((0,), (0,)), ((), ()))


def _sigmoid(x):
    return 0.5 + 0.5 * jnp.tanh(0.5 * x)


def _silu(x, scale=1.0):
    return (x * (0.5 * scale)) * (1.0 + jnp.tanh(0.5 * x))


def _gelu_tanh(x):
    return 0.5 * x * (1.0 + jnp.tanh(0.7978845608028654 * (x + 0.044715 * (x * x * x))))


def _rms_rows(x, w):
    ms = jnp.mean(x * x, axis=-1, keepdims=True)
    return x * lax.rsqrt(ms + EPS) * w


def _params(n_axes):
    return pltpu.CompilerParams(dimension_semantics=("arbitrary",) * n_axes,
                                vmem_limit_bytes=VMEM_LIMIT)


ROW_STEP = 64


def _norm_rows_to(x_ref, w_ref, h_ref, tm):
    def step(i, _):
        rows = pl.ds(pl.multiple_of(i * ROW_STEP, ROW_STEP), ROW_STEP)
        h_ref[rows, :] = _rms_rows(x_ref[rows, :], w_ref[...]).astype(h_ref.dtype)
        return 0
    lax.fori_loop(0, tm // ROW_STEP, step, 0)


def _ffn_body(x_ref, nw_ref, wg_ref, wu_ref, wd_ref, fnw_ref, o_ref, h_ref, *, tm, nk, final):
    k = pl.program_id(1)

    @pl.when(k == 0)
    def _():
        _norm_rows_to(x_ref, nw_ref, h_ref, tm)
        o_ref[...] = jnp.zeros_like(o_ref)

    h = h_ref[...]
    g = jnp.dot(h, wg_ref[...], preferred_element_type=F32)
    u = jnp.dot(h, wu_ref[...], preferred_element_type=F32)
    a = (_silu(g) * u).astype(BF16)
    o_ref[...] += jnp.dot(a, wd_ref[...], preferred_element_type=F32)

    @pl.when(k == nk - 1)
    def _():
        def step(i, _):
            rows = pl.ds(pl.multiple_of(i * ROW_STEP, ROW_STEP), ROW_STEP)
            y = x_ref[rows, :] + 0.5 * o_ref[rows, :]
            if final:
                y = _rms_rows(y, fnw_ref[...])
            o_ref[rows, :] = y
            return 0
        lax.fori_loop(0, tm // ROW_STEP, step, 0)


def _ffn(x, nw, wg, wu, wd, fnw, *, tm, tf, final):
    m = x.shape[0]
    nk = D_FF // tf
    body = functools.partial(_ffn_body, tm=tm, nk=nk, final=final)
    return pl.pallas_call(
        body,
        grid=(m // tm, nk),
        in_specs=[
            pl.BlockSpec((tm, D_MODEL), lambda i, k: (i, 0)),
            pl.BlockSpec((1, D_MODEL), lambda i, k: (0, 0)),
            pl.BlockSpec((D_MODEL, tf), lambda i, k: (0, k)),
            pl.BlockSpec((D_MODEL, tf), lambda i, k: (0, k)),
            pl.BlockSpec((tf, D_MODEL), lambda i, k: (k, 0)),
            pl.BlockSpec((1, D_MODEL), lambda i, k: (0, 0)),
        ],
        out_specs=pl.BlockSpec((tm, D_MODEL), lambda i, k: (i, 0)),
        out_shape=jax.ShapeDtypeStruct((m, D_MODEL), F32),
        scratch_shapes=[pltpu.VMEM((tm, D_MODEL), BF16)],
        compiler_params=_params(2),
        name="ffn",
    )(x, nw, wg, wu, wd, fnw)


def _normproj_body(x_ref, nw_ref, w_ref, o_ref, h_ref, *, tm):
    @pl.when(pl.program_id(1) == 0)
    def _():
        _norm_rows_to(x_ref, nw_ref, h_ref, tm)

    o_ref[...] = jnp.dot(h_ref[...], w_ref[...], preferred_element_type=F32)


def _normproj(x, nw, w, *, tm, tn):
    m = x.shape[0]
    n = w.shape[1]
    return pl.pallas_call(
        functools.partial(_normproj_body, tm=tm),
        grid=(m // tm, n // tn),
        in_specs=[
            pl.BlockSpec((tm, D_MODEL), lambda i, j: (i, 0)),
            pl.BlockSpec((1, D_MODEL), lambda i, j: (0, 0)),
            pl.BlockSpec((D_MODEL, tn), lambda i, j: (0, j)),
        ],
        out_specs=pl.BlockSpec((tm, tn), lambda i, j: (i, j)),
        out_shape=jax.ShapeDtypeStruct((m, n), F32),
        scratch_shapes=[pltpu.VMEM((tm, D_MODEL), BF16)],
        compiler_params=_params(2),
        name="normproj",
    )(x, nw, w)


def _projout_body(*refs, n_in):
    ys = refs[:n_in]
    ws = refs[n_in:2 * n_in]
    res_ref = refs[2 * n_in]
    o_ref = refs[2 * n_in + 1]
    acc = res_ref[...]
    for y_ref, w_ref in zip(ys, ws):
        acc = acc + jnp.dot(y_ref[...], w_ref[...], preferred_element_type=F32)
    o_ref[...] = acc


def _projout(ys, w, res, *, tm):
    m = res.shape[0]
    n_in = len(ys)
    kw = ys[0].shape[1]
    in_specs = [pl.BlockSpec((tm, kw), lambda i: (i, 0)) for _ in ys]
    in_specs += [pl.BlockSpec((kw, D_MODEL), functools.partial(lambda i, j: (j, 0), j=j))
                 for j in range(n_in)]
    in_specs += [pl.BlockSpec((tm, D_MODEL), lambda i: (i, 0))]
    return pl.pallas_call(
        functools.partial(_projout_body, n_in=n_in),
        grid=(m // tm,),
        in_specs=in_specs,
        out_specs=pl.BlockSpec((tm, D_MODEL), lambda i: (i, 0)),
        out_shape=jax.ShapeDtypeStruct((m, D_MODEL), F32),
        compiler_params=_params(1),
        name="projout",
    )(*ys, *([w] * n_in), res)


def _rglru_body(xa_ref, ga_ref, buf0_ref, h0_ref, cw_ref, cb_ref, wa_ref, ba_ref, wx_ref, bx_ref,
                lam_ref, ya_ref, nbuf_ref, nh_ref, ext_ref, a_ref, b_ref, hc_ref, *, tb, nt):
    t = pl.program_id(1)
    halo = SUBLANES

    @pl.when(t == 0)
    def _():
        ext_ref[0:halo, :] = jnp.zeros((halo, LRU_W), F32)
        ext_ref[halo - (CONV_W - 1):halo, :] = buf0_ref[0]
        hc_ref[...] = h0_ref[0]

    @pl.when(t > 0)
    def _():
        ext_ref[0:halo, :] = ext_ref[tb:tb + halo, :]

    x = xa_ref[...]
    ext_ref[halo:halo + tb, :] = x
    xc = ext_ref[halo - 3:halo - 3 + tb, :] * cw_ref[0:1, :]
    xc = xc + ext_ref[halo - 2:halo - 2 + tb, :] * cw_ref[1:2, :]
    xc = xc + ext_ref[halo - 1:halo - 1 + tb, :] * cw_ref[2:3, :]
    xc = xc + x * cw_ref[3:4, :]
    xc = cb_ref[...] + xc

    for n in range(LRU_BLOCKS):
        sl = slice(n * LRU_BS, (n + 1) * LRU_BS)
        xcn = xc[:, sl]
        xb = xcn.astype(BF16)
        r = _sigmoid(jnp.dot(xb, wa_ref[n], preferred_element_type=F32) + ba_ref[:, sl])
        ig = _sigmoid(jnp.dot(xb, wx_ref[n], preferred_element_type=F32) + bx_ref[:, sl])
        nlam = -lam_ref[:, sl]
        softplus = jnp.maximum(nlam, 0.0) + jnp.log(1.0 + jnp.exp(-jnp.abs(nlam)))
        log_a = (-RG_C) * r * softplus
        a = jnp.exp(log_a)
        mult = jnp.sqrt(-jnp.tanh(log_a) * (a * a + 1.0))
        a_ref[:, sl] = a
        b_ref[:, sl] = mult * (ig * xcn)

    rowi = lax.broadcasted_iota(jnp.int32, (SUBLANES, LRU_W), 0)

    def group(gi, carry):
        rows = pl.ds(pl.multiple_of(gi * SUBLANES, SUBLANES), SUBLANES)
        av = a_ref[rows, :]
        bv = b_ref[rows, :]
        for s in (1, 2, 4):
            m = rowi >= s
            ash = pltpu.roll(av, s, 0)
            bsh = pltpu.roll(bv, s, 0)
            bv = jnp.where(m, av * bsh + bv, bv)
            av = jnp.where(m, av * ash, av)
        hrows = av * carry + bv
        b_ref[rows, :] = hrows
        return hrows[SUBLANES - 1:SUBLANES, :]

    carry = lax.fori_loop(0, tb // SUBLANES, group, hc_ref[...])
    hc_ref[...] = carry

    ya_ref[...] = (b_ref[...] * _gelu_tanh(ga_ref[...])).astype(ya_ref.dtype)

    @pl.when(t == nt - 1)
    def _():
        nbuf_ref[0] = ext_ref[halo + tb - (CONV_W - 1):halo + tb, :]
        nh_ref[0] = carry


def _rglru(z, buf0, h0, cw, cb, wa, ba, wx, bx, lam, *, nb, t_len, tb):
    nt = t_len // tb
    row1 = lambda b, t: (0, 0)
    return pl.pallas_call(
        functools.partial(_rglru_body, tb=tb, nt=nt),
        grid=(nb, nt),
        in_specs=[
            pl.BlockSpec((tb, LRU_W), lambda b, t: (b * nt + t, 0)),
            pl.BlockSpec((tb, LRU_W), lambda b, t: (b * nt + t, 1)),
            pl.BlockSpec((1, CONV_W - 1, LRU_W), lambda b, t: (b, 0, 0)),
            pl.BlockSpec((1, 1, LRU_W), lambda b, t: (b, 0, 0)),
            pl.BlockSpec((CONV_W, LRU_W), row1),
            pl.BlockSpec((1, LRU_W), row1),
            pl.BlockSpec((LRU_BLOCKS, LRU_BS, LRU_BS), lambda b, t: (0, 0, 0)),
            pl.BlockSpec((1, LRU_W), row1),
            pl.BlockSpec((LRU_BLOCKS, LRU_BS, LRU_BS), lambda b, t: (0, 0, 0)),
            pl.BlockSpec((1, LRU_W), row1),
            pl.BlockSpec((1, LRU_W), row1),
        ],
        out_specs=[
            pl.BlockSpec((tb, LRU_W), lambda b, t: (b * nt + t, 0)),
            pl.BlockSpec((1, CONV_W - 1, LRU_W), lambda b, t: (b, 0, 0)),
            pl.BlockSpec((1, 1, LRU_W), lambda b, t: (b, 0, 0)),
        ],
        out_shape=[
            jax.ShapeDtypeStruct((nb * t_len, LRU_W), BF16),
            jax.ShapeDtypeStruct((nb, CONV_W - 1, LRU_W), F32),
            jax.ShapeDtypeStruct((nb, 1, LRU_W), F32),
        ],
        scratch_shapes=[
            pltpu.VMEM((tb + SUBLANES, LRU_W), F32),
            pltpu.VMEM((tb, LRU_W), F32),
            pltpu.VMEM((tb, LRU_W), F32),
            pltpu.VMEM((1, LRU_W), F32),
        ],
        compiler_params=_params(2),
        name="rglru",
    )(z, z, buf0, h0, cw, cb, wa, ba, wx, bx, lam)


def _retention_body(q_ref, k_ref, v_ref, g_ref, cos_ref, sin_ref, s0_ref, lg_ref, nw_ref,
                    y_ref, so_ref, s_ref, *, lb, nsub, nt, chunk, hb):
    t = pl.program_id(2)

    @pl.when(t == 0)
    def _():
        s_ref[...] = s0_ref[0]

    ti = lax.broadcasted_iota(jnp.int32, (lb, lb), 0)
    si = lax.broadcasted_iota(jnp.int32, (lb, lb), 1)
    dt = (ti - si).astype(F32)
    tc = ti & (-chunk)
    sc = si & (-chunk)
    same = tc == sc
    earlier = sc < tc
    adt = jnp.abs(dt)
    pdt = jnp.maximum(dt, 0.0)
    rowf = lax.broadcasted_iota(jnp.int32, (lb, RET_DK), 0).astype(F32)
    scale = RET_DK ** -0.5

    consts = []
    for h in range(hb):
        lg = lg_ref[h][:, 0:1]
        dmask = jnp.where(same, jnp.exp(lg * adt), jnp.where(earlier, jnp.exp(lg * pdt), 0.0))
        qdec = jnp.exp(lg * (rowf + 1.0))
        kdec = jnp.exp(lg * ((lb - 1.0) - rowf))
        cdec = jnp.exp(lg * float(lb))
        consts.append((dmask, qdec, kdec, cdec))

    def sub(j, _):
        rows = pl.ds(pl.multiple_of(j * lb, lb), lb)
        cs = cos_ref[rows, :]
        sn = sin_ref[rows, :]
        for h in range(hb):
            dmask, qdec, kdec, cdec = consts[h]
            cols = slice(h * RET_DK, (h + 1) * RET_DK)
            q = q_ref[rows, cols]
            k = k_ref[rows, cols]
            v = v_ref[rows, cols].astype(BF16)
            qr = (q * cs + pltpu.roll(q, RET_DK // 2, 1) * sn) * scale
            kr = k * cs + pltpu.roll(k, RET_DK // 2, 1) * sn
            att = lax.dot_general(qr.astype(BF16), kr.astype(BF16), NT_DIMS,
                                  preferred_element_type=F32) * dmask
            s = s_ref[h]
            o = jnp.dot(att.astype(BF16), v, preferred_element_type=F32)
            o = o + jnp.dot((qr * qdec).astype(BF16), s.astype(BF16), preferred_element_type=F32)
            s_ref[h] = cdec * s + lax.dot_general((kr * kdec).astype(BF16), v, TN_DIMS,
                                                  preferred_element_type=F32)
            yn = o * lax.rsqrt(jnp.mean(o * o, axis=-1, keepdims=True) + EPS)
            y_ref[rows, cols] = (yn * nw_ref[:, cols] * _silu(g_ref[rows, cols])).astype(y_ref.dtype)
        return 0

    lax.fori_loop(0, nsub, sub, 0)

    @pl.when(t == nt - 1)
    def _():
        so_ref[0] = s_ref[...]


def _retention(z, cosf, sinf, s0, lgt, nw, *, nb, t_len, tc, lb, chunk, shared_pos, hb):
    nt = t_len // tc
    wb = hb * RET_DK
    col0 = 2 * LRU_W // wb
    ngrp = RET_H // hb
    pos_map = (lambda b, h, t: (0, 0)) if shared_pos else (lambda b, h, t: (b * nt + t, 0))

    def zspec(off):
        return pl.BlockSpec((tc, wb), lambda b, h, t: (b * nt + t, col0 + off * ngrp + h))

    return pl.pallas_call(
        functools.partial(_retention_body, lb=lb, nsub=tc // lb, nt=nt, chunk=chunk, hb=hb),
        grid=(nb, ngrp, nt),
        in_specs=[
            zspec(0), zspec(1), zspec(2), zspec(3),
            pl.BlockSpec((tc, RET_DK), pos_map),
            pl.BlockSpec((tc, RET_DK), pos_map),
            pl.BlockSpec((1, hb, RET_DK, RET_DK), lambda b, h, t: (b, h, 0, 0)),
            pl.BlockSpec((hb, 1, RET_DK), lambda b, h, t: (h, 0, 0)),
            pl.BlockSpec((1, wb), lambda b, h, t: (0, h)),
        ],
        out_specs=[
            pl.BlockSpec((tc, wb), lambda b, h, t: (b * nt + t, h)),
            pl.BlockSpec((1, hb, RET_DK, RET_DK), lambda b, h, t: (b, h, 0, 0)),
        ],
        out_shape=[
            jax.ShapeDtypeStruct((nb * t_len, RET_H * RET_DK), BF16),
            jax.ShapeDtypeStruct((nb, RET_H, RET_DK, RET_DK), F32),
        ],
        scratch_shapes=[pltpu.VMEM((hb, RET_DK, RET_DK), F32)],
        compiler_params=_params(3),
        name="retention",
    )(z, z, z, z, cosf, sinf, s0, lgt, nw)


def _hgrn_body(q_ref, f_ref, v_ref, g_ref, lbl_ref, nw_ref, s0_ref, y_ref, so_ref, st_ref,
               *, chunk, nsub, nt, layer, hb):
    t = pl.program_id(2)
    L = chunk

    @pl.when(t == 0)
    def _():
        for h in range(hb):
            st_ref[h] = s0_ref[0, h].T

    lgt = lbl_ref[...]
    mx = jnp.max(lgt, axis=0, keepdims=True)
    ex = jnp.exp(lgt - mx)
    p = ex / jnp.sum(ex, axis=0, keepdims=True)
    lb_all = jnp.sum(p[0:layer + 1, :], axis=0, keepdims=True) - p[0:1, :]

    rowi = lax.broadcasted_iota(jnp.int32, (L, HG_DK), 0)
    ti = lax.broadcasted_iota(jnp.int32, (L, L), 0)
    si = lax.broadcasted_iota(jnp.int32, (L, L), 1)
    txs = ti ^ si
    below = si < ti
    eye = ti == si
    level_mask = {}
    blk = L
    while blk >= 2:
        level_mask[blk] = below & (txs >= blk // 2) & (txs < blk)
        blk //= 2
    upper = {}
    sign = {}
    blk = L
    while blk >= SUBLANES:
        upper[blk] = (rowi & (blk - 1)) >= blk // 2
        sign[blk] = jnp.where(upper[blk], 1.0, -1.0)
        blk //= 2
    i4 = rowi & 3
    odd = (rowi & 1) == 1
    row8 = lax.broadcasted_iota(jnp.int32, (SUBLANES, HG_DK), 0)
    nreg = L // SUBLANES
    scale = HG_DK ** -0.5

    def gram(x):
        xb = x.astype(BF16)
        return lax.dot_general(xb, xb, NT_DIMS, preferred_element_type=F32)

    def per_reg(x, fn):
        return jnp.concatenate([fn(x[i * SUBLANES:(i + 1) * SUBLANES, :]) for i in range(nreg)], axis=0)

    def running_sum(x):
        parts = [x[i * SUBLANES:(i + 1) * SUBLANES, :] for i in range(nreg)]
        for s in (1, 2, 4):
            parts = [p + jnp.where(row8 >= s, pltpu.roll(p, s, 0), 0.0) for p in parts]
        out = []
        carry = None
        for p in parts:
            if carry is not None:
                p = p + carry
            out.append(p)
            carry = p[SUBLANES - 1:SUBLANES, :]
        return jnp.concatenate(out, axis=0)

    def one_head(rows, h):
        cols = slice(h * HG_DK, (h + 1) * HG_DK)
        lbv = lb_all[:, cols]
        f = lbv + (1.0 - lbv) * _sigmoid(f_ref[rows, cols])
        qh = _silu(q_ref[rows, cols], scale)
        kk = 1.0 - f
        v = v_ref[rows, cols].astype(BF16)
        c = running_sum(jnp.log2(f))
        clast = c[L - 1:L, :]

        att = jnp.where(eye, jnp.sum(qh * kk, axis=-1, keepdims=True), 0.0)
        blk = L
        while blk >= SUBLANES:
            half = blk // 2
            r = jnp.concatenate(
                [jnp.broadcast_to(c[i * blk + half - 1:i * blk + half, :], (blk, HG_DK))
                 for i in range(L // blk)], axis=0)
            x = jnp.where(upper[blk], qh, kk) * jnp.exp2((c - r) * sign[blk])
            att = jnp.where(level_mask[blk], gram(x), att)
            blk = half
        fprev = per_reg(f, lambda p: pltpu.roll(p, 1, 0))
        fnext = per_reg(f, lambda p: pltpu.roll(p, SUBLANES - 1, 0))
        qf = qh * f
        x4 = jnp.where(i4 == 0, kk * fnext, jnp.where(i4 == 1, kk, jnp.where(i4 == 2, qf, qf * fprev)))
        att = jnp.where(level_mask[4], gram(x4), att)
        att = jnp.where(level_mask[2], gram(jnp.where(odd, qf, kk)), att)

        st = st_ref[h]
        o = jnp.dot(att.astype(BF16), v, preferred_element_type=F32)
        o = o + lax.dot_general((qh * jnp.exp2(c)).astype(BF16), st.astype(BF16), NT_DIMS,
                                preferred_element_type=F32)
        kd = (kk * jnp.exp2(clast - c)).astype(BF16)
        st_ref[h] = jnp.exp2(clast) * st + lax.dot_general(v, kd, TN_DIMS, preferred_element_type=F32)
        yn = o * lax.rsqrt(jnp.mean(o * o, axis=-1, keepdims=True) + EPS)
        y_ref[rows, cols] = (yn * nw_ref[:, cols] * _silu(g_ref[rows, cols])).astype(y_ref.dtype)

    def sub(j, _):
        rows = pl.ds(pl.multiple_of(j * L, L), L)
        for h in range(hb):
            one_head(rows, h)
        return 0

    lax.fori_loop(0, nsub, sub, 0)

    @pl.when(t == nt - 1)
    def _():
        for h in range(hb):
            so_ref[0, h] = st_ref[h].T


def _hgrn(z, logits, nw, s0, *, nb, t_len, tc, chunk, layer, hb):
    nt = t_len // tc
    wb = hb * HG_DK
    ngrp = HG_H // hb

    def zspec(off):
        return pl.BlockSpec((tc, wb), lambda b, h, t: (b * nt + t, off * ngrp + h))

    return pl.pallas_call(
        functools.partial(_hgrn_body, chunk=chunk, nsub=tc // chunk, nt=nt, layer=layer, hb=hb),
        grid=(nb, ngrp, nt),
        in_specs=[
            zspec(0), zspec(1), zspec(2), zspec(3),
            pl.BlockSpec((logits.shape[0], wb), lambda b, h, t: (0, h)),
            pl.BlockSpec((1, wb), lambda b, h, t: (0, h)),
            pl.BlockSpec((1, hb, HG_DK, HG_DK), lambda b, h, t: (b, h, 0, 0)),
        ],
        out_specs=[
            pl.BlockSpec((tc, wb), lambda b, h, t: (b * nt + t, h)),
            pl.BlockSpec((1, hb, HG_DK, HG_DK), lambda b, h, t: (b, h, 0, 0)),
        ],
        out_shape=[
            jax.ShapeDtypeStruct((nb * t_len, HG_H * HG_DK), BF16),
            jax.ShapeDtypeStruct((nb, HG_H, HG_DK, HG_DK), F32),
        ],
        scratch_shapes=[pltpu.VMEM((hb, HG_DK, HG_DK), F32)],
        compiler_params=_params(3),
        name="hgrn",
    )(z, z, z, z, logits, nw, s0)


def _rope_tables(pos):
    half = RET_DK // 2
    inv = ROPE_BASE ** (-jnp.arange(half, dtype=F32) / half)
    ang = pos[:, None] * inv[None, :]
    cos = jnp.cos(ang)
    sin = jnp.sin(ang)
    return jnp.concatenate([cos, cos], axis=-1), jnp.concatenate([-sin, sin], axis=-1)


def _trunk(x, conv0, lru0, ret0, hg0, pos0, w, *, nb, t_len, tm, tb, ret_tc, ret_lb, ret_hb,
           hg_tc, hg_hb):
    chunk = min(t_len, CHUNK)
    cosf, sinf = _rope_tables(pos0 + jnp.arange(t_len, dtype=F32))
    log_g = jnp.log1p(-jnp.exp2(-5.0 - jnp.arange(RET_H, dtype=F32)))
    lgt = jnp.broadcast_to(log_g[:, None, None], (RET_H, 1, RET_DK))
    row = lambda a: a.reshape(1, -1)

    x = _ffn(x, row(w["ffn1_norm"][0]), w["ffn1_wg"][0], w["ffn1_wu"][0], w["ffn1_wd"][0],
             row(w["final_norm"]), tm=tm, tf=512, final=False)
    z = _normproj(x, row(w["mix_norm"][0]), w["ab_w_in"][0], tm=tm, tn=512)
    ya, nbuf, nh = _rglru(z, conv0, lru0.reshape(nb, 1, LRU_W), w["ab_conv_w"][0], row(w["ab_conv_b"][0]),
                          w["ab_gate_a_w"][0], row(w["ab_gate_a_b"][0]), w["ab_gate_x_w"][0],
                          row(w["ab_gate_x_b"][0]), row(w["ab_lru_lambda"][0]),
                          nb=nb, t_len=t_len, tb=tb)
    yr, ns = _retention(z, cosf, sinf, ret0, lgt, row(w["ab_ret_norm"][0]), nb=nb, t_len=t_len,
                        tc=ret_tc, lb=ret_lb, chunk=chunk, shared_pos=nb > 1, hb=ret_hb)
    x = _projout([ya, yr], w["ab_w_out"][0], x, tm=min(tm, 512))
    x = _ffn(x, row(w["ffn2_norm"][0]), w["ffn2_wg"][0], w["ffn2_wu"][0], w["ffn2_wd"][0],
             row(w["final_norm"]), tm=tm, tf=512, final=False)

    x = _ffn(x, row(w["ffn1_norm"][1]), w["ffn1_wg"][1], w["ffn1_wu"][1], w["ffn1_wd"][1],
             row(w["final_norm"]), tm=tm, tf=512, final=False)
    z = _normproj(x, row(w["mix_norm"][1]), w["c_w_in"][0], tm=tm, tn=512)
    yc, nhg = _hgrn(z, w["c_lb_logits"], row(w["c_norm"][0]), hg0, nb=nb, t_len=t_len, tc=hg_tc,
                    chunk=chunk, layer=1, hb=hg_hb)
    x = _projout([yc], w["c_w_out"][0], x, tm=min(tm, 512))
    y = _ffn(x, row(w["ffn2_norm"][1]), w["ffn2_wg"][1], w["ffn2_wu"][1], w["ffn2_wd"][1],
             row(w["final_norm"]), tm=tm, tf=512, final=True)
    return (y.reshape(nb, t_len, D_MODEL), nbuf[None], nh.reshape(1, nb, LRU_W), ns[None], nhg[None])


_MATMUL_WEIGHTS = ("ffn1_wg", "ffn1_wu", "ffn1_wd", "ffn2_wg", "ffn2_wu", "ffn2_wd",
                   "ab_w_in", "ab_gate_a_w", "ab_gate_x_w", "ab_w_out", "c_w_in", "c_w_out")


def kernel(x_prompt, x_sample, state_conv, state_lru, state_ret, state_hgrn, ffn1_norm, ffn1_wg, ffn1_wu, ffn1_wd, mix_norm, ffn2_norm, ffn2_wg, ffn2_wu, ffn2_wd, final_norm, ab_w_in, ab_conv_w, ab_conv_b, ab_gate_a_w, ab_gate_a_b, ab_gate_x_w, ab_gate_x_b, ab_lru_lambda, ab_ret_norm, ab_w_out, c_w_in, c_lb_logits, c_norm, c_w_out):
    w = dict(ffn1_norm=ffn1_norm, ffn1_wg=ffn1_wg, ffn1_wu=ffn1_wu, ffn1_wd=ffn1_wd, mix_norm=mix_norm,
             ffn2_norm=ffn2_norm, ffn2_wg=ffn2_wg, ffn2_wu=ffn2_wu, ffn2_wd=ffn2_wd,
             final_norm=final_norm, ab_w_in=ab_w_in, ab_conv_w=ab_conv_w, ab_conv_b=ab_conv_b,
             ab_gate_a_w=ab_gate_a_w, ab_gate_a_b=ab_gate_a_b, ab_gate_x_w=ab_gate_x_w,
             ab_gate_x_b=ab_gate_x_b, ab_lru_lambda=ab_lru_lambda, ab_ret_norm=ab_ret_norm,
             ab_w_out=ab_w_out, c_w_in=c_w_in, c_lb_logits=c_lb_logits, c_norm=c_norm, c_w_out=c_w_out)
    for name in _MATMUL_WEIGHTS:
        w[name] = w[name].astype(BF16)

    nbp, tp, _ = x_prompt.shape
    nbs, ts, _ = x_sample.shape
    zc = jnp.zeros((nbp, CONV_W - 1, LRU_W), F32)
    zl = jnp.zeros((nbp, LRU_W), F32)
    zr = jnp.zeros((nbp, RET_H, RET_DK, RET_DK), F32)
    zh = jnp.zeros((nbp, HG_H, HG_DK, HG_DK), F32)
    yp, cp, lp, rp, hp = _trunk(x_prompt.reshape(nbp * tp, D_MODEL), zc, zl, zr, zh, 0.0, w,
                                nb=nbp, t_len=tp, tm=1024, tb=256, ret_tc=1024, ret_lb=256, ret_hb=8,
                                hg_tc=256, hg_hb=16)
    ys, cs, ls, rs, hs = _trunk(x_sample.reshape(nbs * ts, D_MODEL), state_conv[0], state_lru[0],
                                state_ret[0], state_hgrn[0], float(PAST_LEN), w,
                                nb=nbs, t_len=ts, tm=nbs * ts, tb=ts, ret_tc=ts, ret_lb=ts, ret_hb=8,
                                hg_tc=ts, hg_hb=8)
    return (yp, ys, cp, lp, rp, hp, cs, ls, rs, hs)
```

```python
import functools

import jax
import jax.numpy as jnp
from jax import lax
from jax.experimental import pallas as pl
from jax.experimental.pallas import tpu as pltpu

F32 = jnp.float32
BF16 = jnp.bfloat16

D_MODEL = 2048
CHUNK = 64
D_FF = 5632
EPS = 1e-6
PAST_LEN = 2048
LRU_W = 1024
LRU_BLOCKS = 8
LRU_BS = 128
CONV_W = 4
RG_C = 8.0
RET_H = 8
RET_DK = 128
ROPE_BASE = 10000.0
HG_H = 16
HG_DK = 128

SUBLANES = 8
LANES = 128
VMEM_LIMIT = 56 * 1024 * 1024

NT_DIMS = (((1,), (1,)), ((), ()))
TN_DIMS = (((0,), (0,)), ((), ()))


def _sigmoid(x):
    return 0.5 + 0.5 * jnp.tanh(0.5 * x)


def _silu(x, scale=1.0):
    return (x * (0.5 * scale)) * (1.0 + jnp.tanh(0.5 * x))


def _gelu_tanh(x):
    return 0.5 * x * (1.0 + jnp.tanh(0.7978845608028654 * (x + 0.044715 * (x * x * x))))


def _rms_rows(x, w):
    ms = jnp.mean(x * x, axis=-1, keepdims=True)
    return x * lax.rsqrt(ms + EPS) * w


def _params(n_axes):
    return pltpu.CompilerParams(dimension_semantics=("arbitrary",) * n_axes,
                                vmem_limit_bytes=VMEM_LIMIT)


ROW_STEP = 64


def _norm_rows_to(x_ref, w_ref, h_ref, tm):
    def step(i, _):
        rows = pl.ds(pl.multiple_of(i * ROW_STEP, ROW_STEP), ROW_STEP)
        h_ref[rows, :] = _rms_rows(x_ref[rows, :], w_ref[...]).astype(h_ref.dtype)
        return 0
    lax.fori_loop(0, tm // ROW_STEP, step, 0)


def _ffn_body(x_ref, nw_ref, wg_ref, wu_ref, wd_ref, fnw_ref, o_ref, h_ref, *, tm, nk, final):
    k = pl.program_id(1)

    @pl.when(k == 0)
    def _():
        _norm_rows_to(x_ref, nw_ref, h_ref, tm)
        o_ref[...] = jnp.zeros_like(o_ref)

    h = h_ref[...]
    g = jnp.dot(h, wg_ref[...], preferred_element_type=F32)
    u = jnp.dot(h, wu_ref[...], preferred_element_type=F32)
    a = (_silu(g) * u).astype(BF16)
    o_ref[...] += jnp.dot(a, wd_ref[...], preferred_element_type=F32)

    @pl.when(k == nk - 1)
    def _():
        def step(i, _):
            rows = pl.ds(pl.multiple_of(i * ROW_STEP, ROW_STEP), ROW_STEP)
            y = x_ref[rows, :] + 0.5 * o_ref[rows, :]
            if final:
                y = _rms_rows(y, fnw_ref[...])
            o_ref[rows, :] = y
            return 0
        lax.fori_loop(0, tm // ROW_STEP, step, 0)


def _ffn(x, nw, wg, wu, wd, fnw, *, layer, tm, tf, final):
    m = x.shape[0]
    nk = D_FF // tf
    body = functools.partial(_ffn_body, tm=tm, nk=nk, final=final)
    return pl.pallas_call(
        body,
        grid=(m // tm, nk),
        in_specs=[
            pl.BlockSpec((tm, D_MODEL), lambda i, k: (i, 0)),
            pl.BlockSpec((1, D_MODEL), lambda i, k: (0, 0)),
            pl.BlockSpec((None, D_MODEL, tf), lambda i, k: (layer, 0, k)),
            pl.BlockSpec((None, D_MODEL, tf), lambda i, k: (layer, 0, k)),
            pl.BlockSpec((None, tf, D_MODEL), lambda i, k: (layer, k, 0)),
            pl.BlockSpec((1, D_MODEL), lambda i, k: (0, 0)),
        ],
        out_specs=pl.BlockSpec((tm, D_MODEL), lambda i, k: (i, 0)),
        out_shape=jax.ShapeDtypeStruct((m, D_MODEL), F32),
        scratch_shapes=[pltpu.VMEM((tm, D_MODEL), BF16)],
        compiler_params=_params(2),
        name="ffn",
    )(x, nw, wg, wu, wd, fnw)


def _normproj_body(x_ref, nw_ref, w_ref, o_ref, h_ref, *, tm):
    @pl.when(pl.program_id(1) == 0)
    def _():
        _norm_rows_to(x_ref, nw_ref, h_ref, tm)

    o_ref[...] = jnp.dot(h_ref[...], w_ref[...], preferred_element_type=F32)


def _normproj(x, nw, w, *, tm, tn):
    m = x.shape[0]
    n = w.shape[1]
    return pl.pallas_call(
        functools.partial(_normproj_body, tm=tm),
        grid=(m // tm, n // tn),
        in_specs=[
            pl.BlockSpec((tm, D_MODEL), lambda i, j: (i, 0)),
            pl.BlockSpec((1, D_MODEL), lambda i, j: (0, 0)),
            pl.BlockSpec((D_MODEL, tn), lambda i, j: (0, j)),
        ],
        out_specs=pl.BlockSpec((tm, tn), lambda i, j: (i, j)),
        out_shape=jax.ShapeDtypeStruct((m, n), F32),
        scratch_shapes=[pltpu.VMEM((tm, D_MODEL), BF16)],
        compiler_params=_params(2),
        name="normproj",
    )(x, nw, w)


def _projout_body(*refs, n_in):
    ys = refs[:n_in]
    ws = refs[n_in:2 * n_in]
    res_ref = refs[2 * n_in]
    o_ref = refs[2 * n_in + 1]
    acc = res_ref[...]
    for y_ref, w_ref in zip(ys, ws):
        acc = acc + jnp.dot(y_ref[...], w_ref[...], preferred_element_type=F32)
    o_ref[...] = acc


def _projout(ys, w, res, *, tm):
    m = res.shape[0]
    n_in = len(ys)
    kw = ys[0].shape[1]
    in_specs = [pl.BlockSpec((tm, kw), lambda i: (i, 0)) for _ in ys]
    in_specs += [pl.BlockSpec((kw, D_MODEL), functools.partial(lambda i, j: (j, 0), j=j))
                 for j in range(n_in)]
    in_specs += [pl.BlockSpec((tm, D_MODEL), lambda i: (i, 0))]
    return pl.pallas_call(
        functools.partial(_projout_body, n_in=n_in),
        grid=(m // tm,),
        in_specs=in_specs,
        out_specs=pl.BlockSpec((tm, D_MODEL), lambda i: (i, 0)),
        out_shape=jax.ShapeDtypeStruct((m, D_MODEL), F32),
        compiler_params=_params(1),
        name="projout",
    )(*ys, *([w] * n_in), res)


def _rglru_body(xa_ref, ga_ref, buf0_ref, h0_ref, cw_ref, cb_ref, wa_ref, ba_ref, wx_ref, bx_ref,
                lam_ref, ya_ref, nbuf_ref, nh_ref, ext_ref, a_ref, b_ref, hc_ref, *, tb, nt):
    t = pl.program_id(1)
    halo = SUBLANES

    @pl.when(t == 0)
    def _():
        ext_ref[0:halo, :] = jnp.zeros((halo, LRU_W), F32)
        ext_ref[halo - (CONV_W - 1):halo, :] = buf0_ref[0]
        hc_ref[...] = h0_ref[0]

    @pl.when(t > 0)
    def _():
        ext_ref[0:halo, :] = ext_ref[tb:tb + halo, :]

    x = xa_ref[...]
    ext_ref[halo:halo + tb, :] = x
    xc = ext_ref[halo - 3:halo - 3 + tb, :] * cw_ref[0:1, :]
    xc = xc + ext_ref[halo - 2:halo - 2 + tb, :] * cw_ref[1:2, :]
    xc = xc + ext_ref[halo - 1:halo - 1 + tb, :] * cw_ref[2:3, :]
    xc = xc + x * cw_ref[3:4, :]
    xc = cb_ref[...] + xc

    for n in range(LRU_BLOCKS):
        sl = slice(n * LRU_BS, (n + 1) * LRU_BS)
        xcn = xc[:, sl]
        xb = xcn.astype(BF16)
        r = _sigmoid(jnp.dot(xb, wa_ref[n], preferred_element_type=F32) + ba_ref[:, sl])
        ig = _sigmoid(jnp.dot(xb, wx_ref[n], preferred_element_type=F32) + bx_ref[:, sl])
        nlam = -lam_ref[:, sl]
        softplus = jnp.maximum(nlam, 0.0) + jnp.log(1.0 + jnp.exp(-jnp.abs(nlam)))
        log_a = (-RG_C) * r * softplus
        a = jnp.exp(log_a)
        mult = jnp.sqrt(-jnp.tanh(log_a) * (a * a + 1.0))
        a_ref[:, sl] = a
        b_ref[:, sl] = mult * (ig * xcn)

    rowi = lax.broadcasted_iota(jnp.int32, (SUBLANES, LRU_W), 0)

    def group(gi, carry):
        rows = pl.ds(pl.multiple_of(gi * SUBLANES, SUBLANES), SUBLANES)
        av = a_ref[rows, :]
        bv = b_ref[rows, :]
        for s in (1, 2, 4):
            m = rowi >= s
            ash = pltpu.roll(av, s, 0)
            bsh = pltpu.roll(bv, s, 0)
            bv = jnp.where(m, av * bsh + bv, bv)
            av = jnp.where(m, av * ash, av)
        hrows = av * carry + bv
        b_ref[rows, :] = hrows
        return hrows[SUBLANES - 1:SUBLANES, :]

    carry = lax.fori_loop(0, tb // SUBLANES, group, hc_ref[...])
    hc_ref[...] = carry

    ya_ref[...] = (b_ref[...] * _gelu_tanh(ga_ref[...])).astype(ya_ref.dtype)

    @pl.when(t == nt - 1)
    def _():
        nbuf_ref[0] = ext_ref[halo + tb - (CONV_W - 1):halo + tb, :]
        nh_ref[0] = carry


def _rglru(z, buf0, h0, cw, cb, wa, ba, wx, bx, lam, *, nb, t_len, tb):
    nt = t_len // tb
    row1 = lambda b, t: (0, 0)
    return pl.pallas_call(
        functools.partial(_rglru_body, tb=tb, nt=nt),
        grid=(nb, nt),
        in_specs=[
            pl.BlockSpec((tb, LRU_W), lambda b, t: (b * nt + t, 0)),
            pl.BlockSpec((tb, LRU_W), lambda b, t: (b * nt + t, 1)),
            pl.BlockSpec((1, CONV_W - 1, LRU_W), lambda b, t: (b, 0, 0)),
            pl.BlockSpec((1, 1, LRU_W), lambda b, t: (b, 0, 0)),
            pl.BlockSpec((CONV_W, LRU_W), row1),
            pl.BlockSpec((1, LRU_W), row1),
            pl.BlockSpec((LRU_BLOCKS, LRU_BS, LRU_BS), lambda b, t: (0, 0, 0)),
            pl.BlockSpec((1, LRU_W), row1),
            pl.BlockSpec((LRU_BLOCKS, LRU_BS, LRU_BS), lambda b, t: (0, 0, 0)),
            pl.BlockSpec((1, LRU_W), row1),
            pl.BlockSpec((1, LRU_W), row1),
        ],
        out_specs=[
            pl.BlockSpec((tb, LRU_W), lambda b, t: (b * nt + t, 0)),
            pl.BlockSpec((1, CONV_W - 1, LRU_W), lambda b, t: (b, 0, 0)),
            pl.BlockSpec((1, 1, LRU_W), lambda b, t: (b, 0, 0)),
        ],
        out_shape=[
            jax.ShapeDtypeStruct((nb * t_len, LRU_W), BF16),
            jax.ShapeDtypeStruct((nb, CONV_W - 1, LRU_W), F32),
            jax.ShapeDtypeStruct((nb, 1, LRU_W), F32),
        ],
        scratch_shapes=[
            pltpu.VMEM((tb + SUBLANES, LRU_W), F32),
            pltpu.VMEM((tb, LRU_W), F32),
            pltpu.VMEM((tb, LRU_W), F32),
            pltpu.VMEM((1, LRU_W), F32),
        ],
        compiler_params=_params(2),
        name="rglru",
    )(z, z, buf0, h0, cw, cb, wa, ba, wx, bx, lam)


def _retention_body(q_ref, k_ref, v_ref, g_ref, cos_ref, sin_ref, s0_ref, lg_ref, nw_ref,
                    y_ref, so_ref, s_ref, *, lb, nsub, nt, chunk, hb):
    t = pl.program_id(2)

    @pl.when(t == 0)
    def _():
        s_ref[...] = s0_ref[0]

    ti = lax.broadcasted_iota(jnp.int32, (lb, lb), 0)
    si = lax.broadcasted_iota(jnp.int32, (lb, lb), 1)
    dt = (ti - si).astype(F32)
    tc = ti & (-chunk)
    sc = si & (-chunk)
    same = tc == sc
    earlier = sc < tc
    adt = jnp.abs(dt)
    pdt = jnp.maximum(dt, 0.0)
    rowf = lax.broadcasted_iota(jnp.int32, (lb, RET_DK), 0).astype(F32)
    scale = RET_DK ** -0.5

    consts = []
    for h in range(hb):
        lg = lg_ref[h][:, 0:1]
        dmask = jnp.where(same, jnp.exp(lg * adt), jnp.where(earlier, jnp.exp(lg * pdt), 0.0))
        qdec = jnp.exp(lg * (rowf + 1.0))
        kdec = jnp.exp(lg * ((lb - 1.0) - rowf))
        cdec = jnp.exp(lg * float(lb))
        consts.append((dmask, qdec, kdec, cdec))

    def sub(j, _):
        rows = pl.ds(pl.multiple_of(j * lb, lb), lb)
        cs = cos_ref[rows, :]
        sn = sin_ref[rows, :]
        for h in range(hb):
            dmask, qdec, kdec, cdec = consts[h]
            cols = slice(h * RET_DK, (h + 1) * RET_DK)
            q = q_ref[rows, cols]
            k = k_ref[rows, cols]
            v = v_ref[rows, cols].astype(BF16)
            qr = (q * cs + pltpu.roll(q, RET_DK // 2, 1) * sn) * scale
            kr = k * cs + pltpu.roll(k, RET_DK // 2, 1) * sn
            att = lax.dot_general(qr.astype(BF16), kr.astype(BF16), NT_DIMS,
                                  preferred_element_type=F32) * dmask
            s = s_ref[h]
            o = jnp.dot(att.astype(BF16), v, preferred_element_type=F32)
            o = o + jnp.dot((qr * qdec).astype(BF16), s.astype(BF16), preferred_element_type=F32)
            s_ref[h] = cdec * s + lax.dot_general((kr * kdec).astype(BF16), v, TN_DIMS,
                                                  preferred_element_type=F32)
            yn = o * lax.rsqrt(jnp.mean(o * o, axis=-1, keepdims=True) + EPS)
            y_ref[rows, cols] = (yn * nw_ref[:, cols] * _silu(g_ref[rows, cols])).astype(y_ref.dtype)
        return 0

    lax.fori_loop(0, nsub, sub, 0)

    @pl.when(t == nt - 1)
    def _():
        so_ref[0] = s_ref[...]


def _retention(z, cosf, sinf, s0, lgt, nw, *, nb, t_len, tc, lb, chunk, shared_pos, hb):
    nt = t_len // tc
    wb = hb * RET_DK
    col0 = 2 * LRU_W // wb
    ngrp = RET_H // hb
    pos_map = (lambda b, h, t: (0, 0)) if shared_pos else (lambda b, h, t: (b * nt + t, 0))

    def zspec(off):
        return pl.BlockSpec((tc, wb), lambda b, h, t: (b * nt + t, col0 + off * ngrp + h))

    return pl.pallas_call(
        functools.partial(_retention_body, lb=lb, nsub=tc // lb, nt=nt, chunk=chunk, hb=hb),
        grid=(nb, ngrp, nt),
        in_specs=[
            zspec(0), zspec(1), zspec(2), zspec(3),
            pl.BlockSpec((tc, RET_DK), pos_map),
            pl.BlockSpec((tc, RET_DK), pos_map),
            pl.BlockSpec((1, hb, RET_DK, RET_DK), lambda b, h, t: (b, h, 0, 0)),
            pl.BlockSpec((hb, 1, RET_DK), lambda b, h, t: (h, 0, 0)),
            pl.BlockSpec((1, wb), lambda b, h, t: (0, h)),
        ],
        out_specs=[
            pl.BlockSpec((tc, wb), lambda b, h, t: (b * nt + t, h)),
            pl.BlockSpec((1, hb, RET_DK, RET_DK), lambda b, h, t: (b, h, 0, 0)),
        ],
        out_shape=[
            jax.ShapeDtypeStruct((nb * t_len, RET_H * RET_DK), BF16),
            jax.ShapeDtypeStruct((nb, RET_H, RET_DK, RET_DK), F32),
        ],
        scratch_shapes=[pltpu.VMEM((hb, RET_DK, RET_DK), F32)],
        compiler_params=_params(3),
        name="retention",
    )(z, z, z, z, cosf, sinf, s0, lgt, nw)


def _hgrn_body(q_ref, f_ref, v_ref, g_ref, lbl_ref, nw_ref, s0_ref, y_ref, so_ref, st_ref,
               *, chunk, nsub, nt, layer, hb):
    t = pl.program_id(2)
    L = chunk

    @pl.when(t == 0)
    def _():
        for h in range(hb):
            st_ref[h] = s0_ref[0, h].T

    lgt = lbl_ref[...]
    mx = jnp.max(lgt, axis=0, keepdims=True)
    ex = jnp.exp(lgt - mx)
    p = ex / jnp.sum(ex, axis=0, keepdims=True)
    lb_all = jnp.sum(p[0:layer + 1, :], axis=0, keepdims=True) - p[0:1, :]

    rowi = lax.broadcasted_iota(jnp.int32, (L, HG_DK), 0)
    ti = lax.broadcasted_iota(jnp.int32, (L, L), 0)
    si = lax.broadcasted_iota(jnp.int32, (L, L), 1)
    txs = ti ^ si
    below = si < ti
    eye = ti == si
    level_mask = {}
    blk = L
    while blk >= 2:
        level_mask[blk] = below & (txs >= blk // 2) & (txs < blk)
        blk //= 2
    upper = {}
    sign = {}
    blk = L
    while blk >= SUBLANES:
        upper[blk] = (rowi & (blk - 1)) >= blk // 2
        sign[blk] = jnp.where(upper[blk], 1.0, -1.0)
        blk //= 2
    i4 = rowi & 3
    odd = (rowi & 1) == 1
    row8 = lax.broadcasted_iota(jnp.int32, (SUBLANES, HG_DK), 0)
    nreg = L // SUBLANES
    scale = HG_DK ** -0.5

    def gram(x):
        xb = x.astype(BF16)
        return lax.dot_general(xb, xb, NT_DIMS, preferred_element_type=F32)

    def per_reg(x, fn):
        return jnp.concatenate([fn(x[i * SUBLANES:(i + 1) * SUBLANES, :]) for i in range(nreg)], axis=0)

    def running_sum(x):
        parts = [x[i * SUBLANES:(i + 1) * SUBLANES, :] for i in range(nreg)]
        for s in (1, 2, 4):
            parts = [p + jnp.where(row8 >= s, pltpu.roll(p, s, 0), 0.0) for p in parts]
        out = []
        carry = None
        for p in parts:
            if carry is not None:
                p = p + carry
            out.append(p)
            carry = p[SUBLANES - 1:SUBLANES, :]
        return jnp.concatenate(out, axis=0)

    def one_head(rows, h):
        cols = slice(h * HG_DK, (h + 1) * HG_DK)
        lbv = lb_all[:, cols]
        f = lbv + (1.0 - lbv) * _sigmoid(f_ref[rows, cols])
        qh = _silu(q_ref[rows, cols], scale)
        kk = 1.0 - f
        v = v_ref[rows, cols].astype(BF16)
        c = running_sum(jnp.log2(f))
        clast = c[L - 1:L, :]

        att = jnp.where(eye, jnp.sum(qh * kk, axis=-1, keepdims=True), 0.0)
        blk = L
        while blk >= SUBLANES:
            half = blk // 2
            r = jnp.concatenate(
                [jnp.broadcast_to(c[i * blk + half - 1:i * blk + half, :], (blk, HG_DK))
                 for i in range(L // blk)], axis=0)
            x = jnp.where(upper[blk], qh, kk) * jnp.exp2((c - r) * sign[blk])
            att = jnp.where(level_mask[blk], gram(x), att)
            blk = half
        fprev = per_reg(f, lambda p: pltpu.roll(p, 1, 0))
        fnext = per_reg(f, lambda p: pltpu.roll(p, SUBLANES - 1, 0))
        qf = qh * f
        x4 = jnp.where(i4 == 0, kk * fnext, jnp.where(i4 == 1, kk, jnp.where(i4 == 2, qf, qf * fprev)))
        att = jnp.where(level_mask[4], gram(x4), att)
        att = jnp.where(level_mask[2], gram(jnp.where(odd, qf, kk)), att)

        st = st_ref[h]
        o = jnp.dot(att.astype(BF16), v, preferred_element_type=F32)
        o = o + lax.dot_general((qh * jnp.exp2(c)).astype(BF16), st.astype(BF16), NT_DIMS,
                                preferred_element_type=F32)
        kd = (kk * jnp.exp2(clast - c)).astype(BF16)
        st_ref[h] = jnp.exp2(clast) * st + lax.dot_general(v, kd, TN_DIMS, preferred_element_type=F32)
        yn = o * lax.rsqrt(jnp.mean(o * o, axis=-1, keepdims=True) + EPS)
        y_ref[rows, cols] = (yn * nw_ref[:, cols] * _silu(g_ref[rows, cols])).astype(y_ref.dtype)

    def sub(j, _):
        rows = pl.ds(pl.multiple_of(j * L, L), L)
        for h in range(hb):
            one_head(rows, h)
        return 0

    lax.fori_loop(0, nsub, sub, 0)

    @pl.when(t == nt - 1)
    def _():
        for h in range(hb):
            so_ref[0, h] = st_ref[h].T


def _hgrn(z, logits, nw, s0, *, nb, t_len, tc, chunk, layer, hb):
    nt = t_len // tc
    wb = hb * HG_DK
    ngrp = HG_H // hb

    def zspec(off):
        return pl.BlockSpec((tc, wb), lambda b, h, t: (b * nt + t, off * ngrp + h))

    return pl.pallas_call(
        functools.partial(_hgrn_body, chunk=chunk, nsub=tc // chunk, nt=nt, layer=layer, hb=hb),
        grid=(nb, ngrp, nt),
        in_specs=[
            zspec(0), zspec(1), zspec(2), zspec(3),
            pl.BlockSpec((logits.shape[0], wb), lambda b, h, t: (0, h)),
            pl.BlockSpec((1, wb), lambda b, h, t: (0, h)),
            pl.BlockSpec((1, hb, HG_DK, HG_DK), lambda b, h, t: (b, h, 0, 0)),
        ],
        out_specs=[
            pl.BlockSpec((tc, wb), lambda b, h, t: (b * nt + t, h)),
            pl.BlockSpec((1, hb, HG_DK, HG_DK), lambda b, h, t: (b, h, 0, 0)),
        ],
        out_shape=[
            jax.ShapeDtypeStruct((nb * t_len, HG_H * HG_DK), BF16),
            jax.ShapeDtypeStruct((nb, HG_H, HG_DK, HG_DK), F32),
        ],
        scratch_shapes=[pltpu.VMEM((hb, HG_DK, HG_DK), F32)],
        compiler_params=_params(3),
        name="hgrn",
    )(z, z, z, z, logits, nw, s0)


def _rope_tables(pos):
    half = RET_DK // 2
    inv = ROPE_BASE ** (-jnp.arange(half, dtype=F32) / half)
    ang = pos[:, None] * inv[None, :]
    cos = jnp.cos(ang)
    sin = jnp.sin(ang)
    return jnp.concatenate([cos, cos], axis=-1), jnp.concatenate([-sin, sin], axis=-1)


def _trunk(x, conv0, lru0, ret0, hg0, pos0, w, *, nb, t_len, tm, tn, tb, ret_tc, ret_lb, ret_hb,
           hg_tc, hg_hb):
    chunk = min(t_len, CHUNK)
    cosf, sinf = _rope_tables(pos0 + jnp.arange(t_len, dtype=F32))
    log_g = jnp.log1p(-jnp.exp2(-5.0 - jnp.arange(RET_H, dtype=F32)))
    lgt = jnp.broadcast_to(log_g[:, None, None], (RET_H, 1, RET_DK))
    row = lambda a: a.reshape(1, -1)

    x = _ffn(x, row(w["ffn1_norm"][0]), w["ffn1_wg"], w["ffn1_wu"], w["ffn1_wd"],
             row(w["final_norm"]), layer=0, tm=tm, tf=512, final=False)
    z = _normproj(x, row(w["mix_norm"][0]), w["ab_w_in"][0], tm=tm, tn=tn)
    ya, nbuf, nh = _rglru(z, conv0, lru0.reshape(nb, 1, LRU_W), w["ab_conv_w"][0], row(w["ab_conv_b"][0]),
                          w["ab_gate_a_w"][0], row(w["ab_gate_a_b"][0]), w["ab_gate_x_w"][0],
                          row(w["ab_gate_x_b"][0]), row(w["ab_lru_lambda"][0]),
                          nb=nb, t_len=t_len, tb=tb)
    yr, ns = _retention(z, cosf, sinf, ret0, lgt, row(w["ab_ret_norm"][0]), nb=nb, t_len=t_len,
                        tc=ret_tc, lb=ret_lb, chunk=chunk, shared_pos=nb > 1, hb=ret_hb)
    x = _projout([ya, yr], w["ab_w_out"][0], x, tm=min(tm, 512))
    x = _ffn(x, row(w["ffn2_norm"][0]), w["ffn2_wg"], w["ffn2_wu"], w["ffn2_wd"],
             row(w["final_norm"]), layer=0, tm=tm, tf=512, final=False)

    x = _ffn(x, row(w["ffn1_norm"][1]), w["ffn1_wg"], w["ffn1_wu"], w["ffn1_wd"],
             row(w["final_norm"]), layer=1, tm=tm, tf=512, final=False)
    z = _normproj(x, row(w["mix_norm"][1]), w["c_w_in"][0], tm=tm, tn=tn)
    yc, nhg = _hgrn(z, w["c_lb_logits"], row(w["c_norm"][0]), hg0, nb=nb, t_len=t_len, tc=hg_tc,
                    chunk=chunk, layer=1, hb=hg_hb)
    x = _projout([yc], w["c_w_out"][0], x, tm=min(tm, 512))
    y = _ffn(x, row(w["ffn2_norm"][1]), w["ffn2_wg"], w["ffn2_wu"], w["ffn2_wd"],
             row(w["final_norm"]), layer=1, tm=tm, tf=512, final=True)
    return (y.reshape(nb, t_len, D_MODEL), nbuf[None], nh.reshape(1, nb, LRU_W), ns[None], nhg[None])


_MATMUL_WEIGHTS = ("ffn1_wg", "ffn1_wu", "ffn1_wd", "ffn2_wg", "ffn2_wu", "ffn2_wd",
                   "ab_w_in", "ab_gate_a_w", "ab_gate_x_w", "ab_w_out", "c_w_in", "c_w_out")


def kernel(x_prompt, x_sample, state_conv, state_lru, state_ret, state_hgrn, ffn1_norm, ffn1_wg, ffn1_wu, ffn1_wd, mix_norm, ffn2_norm, ffn2_wg, ffn2_wu, ffn2_wd, final_norm, ab_w_in, ab_conv_w, ab_conv_b, ab_gate_a_w, ab_gate_a_b, ab_gate_x_w, ab_gate_x_b, ab_lru_lambda, ab_ret_norm, ab_w_out, c_w_in, c_lb_logits, c_norm, c_w_out):
    w = dict(ffn1_norm=ffn1_norm, ffn1_wg=ffn1_wg, ffn1_wu=ffn1_wu, ffn1_wd=ffn1_wd, mix_norm=mix_norm,
             ffn2_norm=ffn2_norm, ffn2_wg=ffn2_wg, ffn2_wu=ffn2_wu, ffn2_wd=ffn2_wd,
             final_norm=final_norm, ab_w_in=ab_w_in, ab_conv_w=ab_conv_w, ab_conv_b=ab_conv_b,
             ab_gate_a_w=ab_gate_a_w, ab_gate_a_b=ab_gate_a_b, ab_gate_x_w=ab_gate_x_w,
             ab_gate_x_b=ab_gate_x_b, ab_lru_lambda=ab_lru_lambda, ab_ret_norm=ab_ret_norm,
             ab_w_out=ab_w_out, c_w_in=c_w_in, c_lb_logits=c_lb_logits, c_norm=c_norm, c_w_out=c_w_out)
    for name in _MATMUL_WEIGHTS:
        w[name] = w[name].astype(BF16)

    nbp, tp, _ = x_prompt.shape
    nbs, ts, _ = x_sample.shape
    zc = jnp.zeros((nbp, CONV_W - 1, LRU_W), F32)
    zl = jnp.zeros((nbp, LRU_W), F32)
    zr = jnp.zeros((nbp, RET_H, RET_DK, RET_DK), F32)
    zh = jnp.zeros((nbp, HG_H, HG_DK, HG_DK), F32)
    yp, cp, lp, rp, hp = _trunk(x_prompt.reshape(nbp * tp, D_MODEL), zc, zl, zr, zh, 0.0, w,
                                nb=nbp, t_len=tp, tm=1024, tn=1024, tb=256, ret_tc=1024, ret_lb=256, ret_hb=8,
                                hg_tc=256, hg_hb=16)
    ys, cs, ls, rs, hs = _trunk(x_sample.reshape(nbs * ts, D_MODEL), state_conv[0], state_lru[0],
                                state_ret[0], state_hgrn[0], float(PAST_LEN), w,
                                nb=nbs, t_len=ts, tm=nbs * ts, tn=1024, tb=ts, ret_tc=ts, ret_lb=ts, ret_hb=8,
                                hg_tc=ts, hg_hb=8)
    return (yp, ys, cp, lp, rp, hp, cs, ls, rs, hs)
```

```python
import functools

import jax
import jax.numpy as jnp
from jax import lax
from jax.experimental import pallas as pl
from jax.experimental.pallas import tpu as pltpu

F32 = jnp.float32
BF16 = jnp.bfloat16

D_MODEL = 2048
CHUNK = 64
D_FF = 5632
EPS = 1e-6
PAST_LEN = 2048
LRU_W = 1024
LRU_BLOCKS = 8
LRU_BS = 128
CONV_W = 4
RG_C = 8.0
RET_H = 8
RET_DK = 128
ROPE_BASE = 10000.0
HG_H = 16
HG_DK = 128

SUBLANES = 8
LANES = 128
VMEM_LIMIT = 56 * 1024 * 1024

NT_DIMS = (((1,), (1,)), ((), ()))
TN_DIMS = (((0,), (0,)), ((), ()))


def _sigmoid(x):
    return 0.5 + 0.5 * jnp.tanh(0.5 * x)


def _silu(x, scale=1.0):
    return (x * (0.5 * scale)) * (1.0 + jnp.tanh(0.5 * x))


def _gelu_tanh(x):
    return 0.5 * x * (1.0 + jnp.tanh(0.7978845608028654 * (x + 0.044715 * (x * x * x))))


def _rms_rows(x, w):
    ms = jnp.mean(x * x, axis=-1, keepdims=True)
    return x * lax.rsqrt(ms + EPS) * w


def _params(n_axes):
    return pltpu.CompilerParams(dimension_semantics=("arbitrary",) * n_axes,
                                vmem_limit_bytes=VMEM_LIMIT)


ROW_STEP = 64


def _norm_rows_to(x_ref, w_ref, h_ref, tm):
    def step(i, _):
        rows = pl.ds(pl.multiple_of(i * ROW_STEP, ROW_STEP), ROW_STEP)
        h_ref[rows, :] = _rms_rows(x_ref[rows, :], w_ref[...]).astype(h_ref.dtype)
        return 0
    lax.fori_loop(0, tm // ROW_STEP, step, 0)


def _ffn_body(x_ref, nw_ref, wg_ref, wu_ref, wd_ref, fnw_ref, o_ref, h_ref, *, tm, nk, final):
    k = pl.program_id(1)

    @pl.when(k == 0)
    def _():
        _norm_rows_to(x_ref, nw_ref, h_ref, tm)
        o_ref[...] = jnp.zeros_like(o_ref)

    h = h_ref[...]
    g = jnp.dot(h, wg_ref[...], preferred_element_type=F32)
    u = jnp.dot(h, wu_ref[...], preferred_element_type=F32)
    a = (_silu(g) * u).astype(BF16)
    o_ref[...] += jnp.dot(a, wd_ref[...], preferred_element_type=F32)

    @pl.when(k == nk - 1)
    def _():
        def step(i, _):
            rows = pl.ds(pl.multiple_of(i * ROW_STEP, ROW_STEP), ROW_STEP)
            y = x_ref[rows, :] + 0.5 * o_ref[rows, :]
            if final:
                y = _rms_rows(y, fnw_ref[...])
            o_ref[rows, :] = y
            return 0
        lax.fori_loop(0, tm // ROW_STEP, step, 0)


def _ffn(x, nw, wg, wu, wd, fnw, *, layer, tm, tf, final):
    m = x.shape[0]
    nk = D_FF // tf
    body = functools.partial(_ffn_body, tm=tm, nk=nk, final=final)
    return pl.pallas_call(
        body,
        grid=(m // tm, nk),
        in_specs=[
            pl.BlockSpec((tm, D_MODEL), lambda i, k: (i, 0)),
            pl.BlockSpec((1, D_MODEL), lambda i, k: (0, 0)),
            pl.BlockSpec((None, D_MODEL, tf), lambda i, k: (layer, 0, k)),
            pl.BlockSpec((None, D_MODEL, tf), lambda i, k: (layer, 0, k)),
            pl.BlockSpec((None, tf, D_MODEL), lambda i, k: (layer, k, 0)),
            pl.BlockSpec((1, D_MODEL), lambda i, k: (0, 0)),
        ],
        out_specs=pl.BlockSpec((tm, D_MODEL), lambda i, k: (i, 0)),
        out_shape=jax.ShapeDtypeStruct((m, D_MODEL), F32),
        scratch_shapes=[pltpu.VMEM((tm, D_MODEL), BF16)],
        compiler_params=_params(2),
        name="ffn",
    )(x, nw, wg, wu, wd, fnw)


def _normproj_body(x_ref, nw_ref, w_ref, o_ref, h_ref, *, tm):
    @pl.when(pl.program_id(1) == 0)
    def _():
        _norm_rows_to(x_ref, nw_ref, h_ref, tm)

    o_ref[...] = jnp.dot(h_ref[...], w_ref[...].astype(BF16), preferred_element_type=F32)


def _normproj(x, nw, w, *, tm, tn):
    m = x.shape[0]
    n = w.shape[1]
    return pl.pallas_call(
        functools.partial(_normproj_body, tm=tm),
        grid=(m // tm, n // tn),
        in_specs=[
            pl.BlockSpec((tm, D_MODEL), lambda i, j: (i, 0)),
            pl.BlockSpec((1, D_MODEL), lambda i, j: (0, 0)),
            pl.BlockSpec((D_MODEL, tn), lambda i, j: (0, j)),
        ],
        out_specs=pl.BlockSpec((tm, tn), lambda i, j: (i, j)),
        out_shape=jax.ShapeDtypeStruct((m, n), F32),
        scratch_shapes=[pltpu.VMEM((tm, D_MODEL), BF16)],
        compiler_params=_params(2),
        name="normproj",
    )(x, nw, w)


def _projout_body(*refs, n_in):
    ys = refs[:n_in]
    ws = refs[n_in:2 * n_in]
    res_ref = refs[2 * n_in]
    o_ref = refs[2 * n_in + 1]
    acc = res_ref[...]
    for y_ref, w_ref in zip(ys, ws):
        acc = acc + jnp.dot(y_ref[...], w_ref[...], preferred_element_type=F32)
    o_ref[...] = acc


def _projout(ys, w, res, *, tm):
    m = res.shape[0]
    n_in = len(ys)
    kw = ys[0].shape[1]
    in_specs = [pl.BlockSpec((tm, kw), lambda i: (i, 0)) for _ in ys]
    in_specs += [pl.BlockSpec((kw, D_MODEL), functools.partial(lambda i, j: (j, 0), j=j))
                 for j in range(n_in)]
    in_specs += [pl.BlockSpec((tm, D_MODEL), lambda i: (i, 0))]
    return pl.pallas_call(
        functools.partial(_projout_body, n_in=n_in),
        grid=(m // tm,),
        in_specs=in_specs,
        out_specs=pl.BlockSpec((tm, D_MODEL), lambda i: (i, 0)),
        out_shape=jax.ShapeDtypeStruct((m, D_MODEL), F32),
        compiler_params=_params(1),
        name="projout",
    )(*ys, *([w] * n_in), res)


def _rglru_body(xa_ref, ga_ref, buf0_ref, h0_ref, cw_ref, cb_ref, wa_ref, ba_ref, wx_ref, bx_ref,
                lam_ref, ya_ref, nbuf_ref, nh_ref, ext_ref, a_ref, b_ref, hc_ref, *, tb, nt):
    t = pl.program_id(1)
    halo = SUBLANES

    @pl.when(t == 0)
    def _():
        ext_ref[0:halo, :] = jnp.zeros((halo, LRU_W), F32)
        ext_ref[halo - (CONV_W - 1):halo, :] = buf0_ref[0]
        hc_ref[...] = h0_ref[0]

    @pl.when(t > 0)
    def _():
        ext_ref[0:halo, :] = ext_ref[tb:tb + halo, :]

    x = xa_ref[...]
    ext_ref[halo:halo + tb, :] = x
    xc = ext_ref[halo - 3:halo - 3 + tb, :] * cw_ref[0:1, :]
    xc = xc + ext_ref[halo - 2:halo - 2 + tb, :] * cw_ref[1:2, :]
    xc = xc + ext_ref[halo - 1:halo - 1 + tb, :] * cw_ref[2:3, :]
    xc = xc + x * cw_ref[3:4, :]
    xc = cb_ref[...] + xc

    for n in range(LRU_BLOCKS):
        sl = slice(n * LRU_BS, (n + 1) * LRU_BS)
        xcn = xc[:, sl]
        xb = xcn.astype(BF16)
        r = _sigmoid(jnp.dot(xb, wa_ref[n], preferred_element_type=F32) + ba_ref[:, sl])
        ig = _sigmoid(jnp.dot(xb, wx_ref[n], preferred_element_type=F32) + bx_ref[:, sl])
        nlam = -lam_ref[:, sl]
        softplus = jnp.maximum(nlam, 0.0) + jnp.log(1.0 + jnp.exp(-jnp.abs(nlam)))
        log_a = (-RG_C) * r * softplus
        a = jnp.exp(log_a)
        mult = jnp.sqrt(-jnp.tanh(log_a) * (a * a + 1.0))
        a_ref[:, sl] = a
        b_ref[:, sl] = mult * (ig * xcn)

    rowi = lax.broadcasted_iota(jnp.int32, (SUBLANES, LRU_W), 0)

    def group(gi, carry):
        rows = pl.ds(pl.multiple_of(gi * SUBLANES, SUBLANES), SUBLANES)
        av = a_ref[rows, :]
        bv = b_ref[rows, :]
        for s in (1, 2, 4):
            m = rowi >= s
            ash = pltpu.roll(av, s, 0)
            bsh = pltpu.roll(bv, s, 0)
            bv = jnp.where(m, av * bsh + bv, bv)
            av = jnp.where(m, av * ash, av)
        hrows = av * carry + bv
        b_ref[rows, :] = hrows
        return hrows[SUBLANES - 1:SUBLANES, :]

    carry = lax.fori_loop(0, tb // SUBLANES, group, hc_ref[...], unroll=4)
    hc_ref[...] = carry

    ya_ref[...] = (b_ref[...] * _gelu_tanh(ga_ref[...])).astype(ya_ref.dtype)

    @pl.when(t == nt - 1)
    def _():
        nbuf_ref[0] = ext_ref[halo + tb - (CONV_W - 1):halo + tb, :]
        nh_ref[0] = carry


def _rglru(z, buf0, h0, cw, cb, wa, ba, wx, bx, lam, *, nb, t_len, tb):
    nt = t_len // tb
    row1 = lambda b, t: (0, 0)
    return pl.pallas_call(
        functools.partial(_rglru_body, tb=tb, nt=nt),
        grid=(nb, nt),
        in_specs=[
            pl.BlockSpec((tb, LRU_W), lambda b, t: (b * nt + t, 0)),
            pl.BlockSpec((tb, LRU_W), lambda b, t: (b * nt + t, 1)),
            pl.BlockSpec((1, CONV_W - 1, LRU_W), lambda b, t: (b, 0, 0)),
            pl.BlockSpec((1, 1, LRU_W), lambda b, t: (b, 0, 0)),
            pl.BlockSpec((CONV_W, LRU_W), row1),
            pl.BlockSpec((1, LRU_W), row1),
            pl.BlockSpec((LRU_BLOCKS, LRU_BS, LRU_BS), lambda b, t: (0, 0, 0)),
            pl.BlockSpec((1, LRU_W), row1),
            pl.BlockSpec((LRU_BLOCKS, LRU_BS, LRU_BS), lambda b, t: (0, 0, 0)),
            pl.BlockSpec((1, LRU_W), row1),
            pl.BlockSpec((1, LRU_W), row1),
        ],
        out_specs=[
            pl.BlockSpec((tb, LRU_W), lambda b, t: (b * nt + t, 0)),
            pl.BlockSpec((1, CONV_W - 1, LRU_W), lambda b, t: (b, 0, 0)),
            pl.BlockSpec((1, 1, LRU_W), lambda b, t: (b, 0, 0)),
        ],
        out_shape=[
            jax.ShapeDtypeStruct((nb * t_len, LRU_W), BF16),
            jax.ShapeDtypeStruct((nb, CONV_W - 1, LRU_W), F32),
            jax.ShapeDtypeStruct((nb, 1, LRU_W), F32),
        ],
        scratch_shapes=[
            pltpu.VMEM((tb + SUBLANES, LRU_W), F32),
            pltpu.VMEM((tb, LRU_W), F32),
            pltpu.VMEM((tb, LRU_W), F32),
            pltpu.VMEM((1, LRU_W), F32),
        ],
        compiler_params=_params(2),
        name="rglru",
    )(z, z, buf0, h0, cw, cb, wa, ba, wx, bx, lam)


def _retention_body(q_ref, k_ref, v_ref, g_ref, cos_ref, sin_ref, s0_ref, lg_ref, nw_ref,
                    y_ref, so_ref, s_ref, *, lb, nsub, nt, chunk, hb):
    t = pl.program_id(2)

    @pl.when(t == 0)
    def _():
        s_ref[...] = s0_ref[0]

    ti = lax.broadcasted_iota(jnp.int32, (lb, lb), 0)
    si = lax.broadcasted_iota(jnp.int32, (lb, lb), 1)
    dt = (ti - si).astype(F32)
    tc = ti & (-chunk)
    sc = si & (-chunk)
    same = tc == sc
    earlier = sc < tc
    adt = jnp.abs(dt)
    pdt = jnp.maximum(dt, 0.0)
    rowf = lax.broadcasted_iota(jnp.int32, (lb, RET_DK), 0).astype(F32)
    scale = RET_DK ** -0.5

    consts = []
    for h in range(hb):
        lg = lg_ref[h][:, 0:1]
        dmask = jnp.where(same, jnp.exp(lg * adt), jnp.where(earlier, jnp.exp(lg * pdt), 0.0))
        qdec = jnp.exp(lg * (rowf + 1.0))
        kdec = jnp.exp(lg * ((lb - 1.0) - rowf))
        cdec = jnp.exp(lg * float(lb))
        consts.append((dmask, qdec, kdec, cdec))

    def sub(j, _):
        rows = pl.ds(pl.multiple_of(j * lb, lb), lb)
        cs = cos_ref[rows, :]
        sn = sin_ref[rows, :]
        for h in range(hb):
            dmask, qdec, kdec, cdec = consts[h]
            cols = slice(h * RET_DK, (h + 1) * RET_DK)
            q = q_ref[rows, cols]
            k = k_ref[rows, cols]
            v = v_ref[rows, cols].astype(BF16)
            qr = (q * cs + pltpu.roll(q, RET_DK // 2, 1) * sn) * scale
            kr = k * cs + pltpu.roll(k, RET_DK // 2, 1) * sn
            att = lax.dot_general(qr.astype(BF16), kr.astype(BF16), NT_DIMS,
                                  preferred_element_type=F32) * dmask
            s = s_ref[h]
            o = jnp.dot(att.astype(BF16), v, preferred_element_type=F32)
            o = o + jnp.dot((qr * qdec).astype(BF16), s.astype(BF16), preferred_element_type=F32)
            s_ref[h] = cdec * s + lax.dot_general((kr * kdec).astype(BF16), v, TN_DIMS,
                                                  preferred_element_type=F32)
            yn = o * lax.rsqrt(jnp.mean(o * o, axis=-1, keepdims=True) + EPS)
            y_ref[rows, cols] = (yn * nw_ref[:, cols] * _silu(g_ref[rows, cols])).astype(y_ref.dtype)
        return 0

    lax.fori_loop(0, nsub, sub, 0)

    @pl.when(t == nt - 1)
    def _():
        so_ref[0] = s_ref[...]


def _retention(z, cosf, sinf, s0, lgt, nw, *, nb, t_len, tc, lb, chunk, shared_pos, hb):
    nt = t_len // tc
    wb = hb * RET_DK
    col0 = 2 * LRU_W // wb
    ngrp = RET_H // hb
    pos_map = (lambda b, h, t: (0, 0)) if shared_pos else (lambda b, h, t: (b * nt + t, 0))

    def zspec(off):
        return pl.BlockSpec((tc, wb), lambda b, h, t: (b * nt + t, col0 + off * ngrp + h))

    return pl.pallas_call(
        functools.partial(_retention_body, lb=lb, nsub=tc // lb, nt=nt, chunk=chunk, hb=hb),
        grid=(nb, ngrp, nt),
        in_specs=[
            zspec(0), zspec(1), zspec(2), zspec(3),
            pl.BlockSpec((tc, RET_DK), pos_map),
            pl.BlockSpec((tc, RET_DK), pos_map),
            pl.BlockSpec((1, hb, RET_DK, RET_DK), lambda b, h, t: (b, h, 0, 0)),
            pl.BlockSpec((hb, 1, RET_DK), lambda b, h, t: (h, 0, 0)),
            pl.BlockSpec((1, wb), lambda b, h, t: (0, h)),
        ],
        out_specs=[
            pl.BlockSpec((tc, wb), lambda b, h, t: (b * nt + t, h)),
            pl.BlockSpec((1, hb, RET_DK, RET_DK), lambda b, h, t: (b, h, 0, 0)),
        ],
        out_shape=[
            jax.ShapeDtypeStruct((nb * t_len, RET_H * RET_DK), BF16),
            jax.ShapeDtypeStruct((nb, RET_H, RET_DK, RET_DK), F32),
        ],
        scratch_shapes=[pltpu.VMEM((hb, RET_DK, RET_DK), F32)],
        compiler_params=_params(3),
        name="retention",
    )(z, z, z, z, cosf, sinf, s0, lgt, nw)


def _hgrn_body(q_ref, f_ref, v_ref, g_ref, lbl_ref, nw_ref, s0_ref, y_ref, so_ref, st_ref,
               *, chunk, nsub, nt, layer, hb):
    t = pl.program_id(2)
    L = chunk

    @pl.when(t == 0)
    def _():
        for h in range(hb):
            st_ref[h] = s0_ref[0, h].T

    lgt = lbl_ref[...]
    mx = jnp.max(lgt, axis=0, keepdims=True)
    ex = jnp.exp(lgt - mx)
    p = ex / jnp.sum(ex, axis=0, keepdims=True)
    lb_all = jnp.sum(p[0:layer + 1, :], axis=0, keepdims=True) - p[0:1, :]

    rowi = lax.broadcasted_iota(jnp.int32, (L, HG_DK), 0)
    ti = lax.broadcasted_iota(jnp.int32, (L, L), 0)
    si = lax.broadcasted_iota(jnp.int32, (L, L), 1)
    txs = ti ^ si
    below = si < ti
    eye = ti == si
    level_mask = {}
    blk = L
    while blk >= 2:
        level_mask[blk] = below & (txs >= blk // 2) & (txs < blk)
        blk //= 2
    upper = {}
    sign = {}
    blk = L
    while blk >= SUBLANES:
        upper[blk] = (rowi & (blk - 1)) >= blk // 2
        sign[blk] = jnp.where(upper[blk], 1.0, -1.0)
        blk //= 2
    i4 = rowi & 3
    odd = (rowi & 1) == 1
    row8 = lax.broadcasted_iota(jnp.int32, (SUBLANES, HG_DK), 0)
    nreg = L // SUBLANES
    scale = HG_DK ** -0.5

    def gram(x):
        xb = x.astype(BF16)
        return lax.dot_general(xb, xb, NT_DIMS, preferred_element_type=F32)

    def per_reg(x, fn):
        return jnp.concatenate([fn(x[i * SUBLANES:(i + 1) * SUBLANES, :]) for i in range(nreg)], axis=0)

    def running_sum(x):
        parts = [x[i * SUBLANES:(i + 1) * SUBLANES, :] for i in range(nreg)]
        for s in (1, 2, 4):
            parts = [p + jnp.where(row8 >= s, pltpu.roll(p, s, 0), 0.0) for p in parts]
        out = []
        carry = None
        for p in parts:
            if carry is not None:
                p = p + carry
            out.append(p)
            carry = p[SUBLANES - 1:SUBLANES, :]
        return jnp.concatenate(out, axis=0)

    def one_head(rows, h):
        cols = slice(h * HG_DK, (h + 1) * HG_DK)
        lbv = lb_all[:, cols]
        f = lbv + (1.0 - lbv) * _sigmoid(f_ref[rows, cols])
        qh = _silu(q_ref[rows, cols], scale)
        kk = 1.0 - f
        v = v_ref[rows, cols].astype(BF16)
        c = running_sum(jnp.log2(f))
        clast = c[L - 1:L, :]

        att = jnp.where(eye, jnp.sum(qh * kk, axis=-1, keepdims=True), 0.0)
        blk = L
        while blk >= SUBLANES:
            half = blk // 2
            r = jnp.concatenate(
                [jnp.broadcast_to(c[i * blk + half - 1:i * blk + half, :], (blk, HG_DK))
                 for i in range(L // blk)], axis=0)
            x = jnp.where(upper[blk], qh, kk) * jnp.exp2((c - r) * sign[blk])
            att = jnp.where(level_mask[blk], gram(x), att)
            blk = half
        fprev = per_reg(f, lambda p: pltpu.roll(p, 1, 0))
        fnext = per_reg(f, lambda p: pltpu.roll(p, SUBLANES - 1, 0))
        qf = qh * f
        x4 = jnp.where(i4 == 0, kk * fnext, jnp.where(i4 == 1, kk, jnp.where(i4 == 2, qf, qf * fprev)))
        att = jnp.where(level_mask[4], gram(x4), att)
        att = jnp.where(level_mask[2], gram(jnp.where(odd, qf, kk)), att)

        st = st_ref[h]
        o = jnp.dot(att.astype(BF16), v, preferred_element_type=F32)
        o = o + lax.dot_general((qh * jnp.exp2(c)).astype(BF16), st.astype(BF16), NT_DIMS,
                                preferred_element_type=F32)
        kd = (kk * jnp.exp2(clast - c)).astype(BF16)
        st_ref[h] = jnp.exp2(clast) * st + lax.dot_general(v, kd, TN_DIMS, preferred_element_type=F32)
        yn = o * lax.rsqrt(jnp.mean(o * o, axis=-1, keepdims=True) + EPS)
        y_ref[rows, cols] = (yn * nw_ref[:, cols] * _silu(g_ref[rows, cols])).astype(y_ref.dtype)

    def sub(j, _):
        rows = pl.ds(pl.multiple_of(j * L, L), L)
        for h in range(hb):
            one_head(rows, h)
        return 0

    lax.fori_loop(0, nsub, sub, 0)

    @pl.when(t == nt - 1)
    def _():
        for h in range(hb):
            so_ref[0, h] = st_ref[h].T


def _hgrn(z, logits, nw, s0, *, nb, t_len, tc, chunk, layer, hb):
    nt = t_len // tc
    wb = hb * HG_DK
    ngrp = HG_H // hb

    def zspec(off):
        return pl.BlockSpec((tc, wb), lambda b, h, t: (b * nt + t, off * ngrp + h))

    return pl.pallas_call(
        functools.partial(_hgrn_body, chunk=chunk, nsub=tc // chunk, nt=nt, layer=layer, hb=hb),
        grid=(nb, ngrp, nt),
        in_specs=[
            zspec(0), zspec(1), zspec(2), zspec(3),
            pl.BlockSpec((logits.shape[0], wb), lambda b, h, t: (0, h)),
            pl.BlockSpec((1, wb), lambda b, h, t: (0, h)),
            pl.BlockSpec((1, hb, HG_DK, HG_DK), lambda b, h, t: (b, h, 0, 0)),
        ],
        out_specs=[
            pl.BlockSpec((tc, wb), lambda b, h, t: (b * nt + t, h)),
            pl.BlockSpec((1, hb, HG_DK, HG_DK), lambda b, h, t: (b, h, 0, 0)),
        ],
        out_shape=[
            jax.ShapeDtypeStruct((nb * t_len, HG_H * HG_DK), BF16),
            jax.ShapeDtypeStruct((nb, HG_H, HG_DK, HG_DK), F32),
        ],
        scratch_shapes=[pltpu.VMEM((hb, HG_DK, HG_DK), F32)],
        compiler_params=_params(3),
        name="hgrn",
    )(z, z, z, z, logits, nw, s0)


def _rope_tables(pos):
    half = RET_DK // 2
    inv = ROPE_BASE ** (-jnp.arange(half, dtype=F32) / half)
    ang = pos[:, None] * inv[None, :]
    cos = jnp.cos(ang)
    sin = jnp.sin(ang)
    return jnp.concatenate([cos, cos], axis=-1), jnp.concatenate([-sin, sin], axis=-1)


def _trunk(x, conv0, lru0, ret0, hg0, pos0, w, *, nb, t_len, tm, tn, tb, ret_tc, ret_lb, ret_hb,
           hg_tc, hg_hb):
    chunk = min(t_len, CHUNK)
    cosf, sinf = _rope_tables(pos0 + jnp.arange(t_len, dtype=F32))
    log_g = jnp.log1p(-jnp.exp2(-5.0 - jnp.arange(RET_H, dtype=F32)))
    lgt = jnp.broadcast_to(log_g[:, None, None], (RET_H, 1, RET_DK))
    row = lambda a: a.reshape(1, -1)

    x = _ffn(x, row(w["ffn1_norm"][0]), w["ffn1_wg"], w["ffn1_wu"], w["ffn1_wd"],
             row(w["final_norm"]), layer=0, tm=tm, tf=512, final=False)
    z = _normproj(x, row(w["mix_norm"][0]), w["ab_w_in"][0], tm=tm, tn=tn)
    ya, nbuf, nh = _rglru(z, conv0, lru0.reshape(nb, 1, LRU_W), w["ab_conv_w"][0], row(w["ab_conv_b"][0]),
                          w["ab_gate_a_w"][0], row(w["ab_gate_a_b"][0]), w["ab_gate_x_w"][0],
                          row(w["ab_gate_x_b"][0]), row(w["ab_lru_lambda"][0]),
                          nb=nb, t_len=t_len, tb=tb)
    yr, ns = _retention(z, cosf, sinf, ret0, lgt, row(w["ab_ret_norm"][0]), nb=nb, t_len=t_len,
                        tc=ret_tc, lb=ret_lb, chunk=chunk, shared_pos=nb > 1, hb=ret_hb)
    x = _projout([ya, yr], w["ab_w_out"][0], x, tm=min(tm, 512))
    x = _ffn(x, row(w["ffn2_norm"][0]), w["ffn2_wg"], w["ffn2_wu"], w["ffn2_wd"],
             row(w["final_norm"]), layer=0, tm=tm, tf=512, final=False)

    x = _ffn(x, row(w["ffn1_norm"][1]), w["ffn1_wg"], w["ffn1_wu"], w["ffn1_wd"],
             row(w["final_norm"]), layer=1, tm=tm, tf=512, final=False)
    z = _normproj(x, row(w["mix_norm"][1]), w["c_w_in"][0], tm=tm, tn=tn)
    yc, nhg = _hgrn(z, w["c_lb_logits"], row(w["c_norm"][0]), hg0, nb=nb, t_len=t_len, tc=hg_tc,
                    chunk=chunk, layer=1, hb=hg_hb)
    x = _projout([yc], w["c_w_out"][0], x, tm=min(tm, 512))
    y = _ffn(x, row(w["ffn2_norm"][1]), w["ffn2_wg"], w["ffn2_wu"], w["ffn2_wd"],
             row(w["final_norm"]), layer=1, tm=tm, tf=512, final=True)
    return (y.reshape(nb, t_len, D_MODEL), nbuf[None], nh.reshape(1, nb, LRU_W), ns[None], nhg[None])


_MATMUL_WEIGHTS = ("ffn1_wg", "ffn1_wu", "ffn1_wd", "ffn2_wg", "ffn2_wu", "ffn2_wd",
                   "ab_gate_a_w", "ab_gate_x_w", "ab_w_out", "c_w_out")


def kernel(x_prompt, x_sample, state_conv, state_lru, state_ret, state_hgrn, ffn1_norm, ffn1_wg, ffn1_wu, ffn1_wd, mix_norm, ffn2_norm, ffn2_wg, ffn2_wu, ffn2_wd, final_norm, ab_w_in, ab_conv_w, ab_conv_b, ab_gate_a_w, ab_gate_a_b, ab_gate_x_w, ab_gate_x_b, ab_lru_lambda, ab_ret_norm, ab_w_out, c_w_in, c_lb_logits, c_norm, c_w_out):
    w = dict(ffn1_norm=ffn1_norm, ffn1_wg=ffn1_wg, ffn1_wu=ffn1_wu, ffn1_wd=ffn1_wd, mix_norm=mix_norm,
             ffn2_norm=ffn2_norm, ffn2_wg=ffn2_wg, ffn2_wu=ffn2_wu, ffn2_wd=ffn2_wd,
             final_norm=final_norm, ab_w_in=ab_w_in, ab_conv_w=ab_conv_w, ab_conv_b=ab_conv_b,
             ab_gate_a_w=ab_gate_a_w, ab_gate_a_b=ab_gate_a_b, ab_gate_x_w=ab_gate_x_w,
             ab_gate_x_b=ab_gate_x_b, ab_lru_lambda=ab_lru_lambda, ab_ret_norm=ab_ret_norm,
             ab_w_out=ab_w_out, c_w_in=c_w_in, c_lb_logits=c_lb_logits, c_norm=c_norm, c_w_out=c_w_out)
    for name in _MATMUL_WEIGHTS:
        w[name] = w[name].astype(BF16)

    nbp, tp, _ = x_prompt.shape
    nbs, ts, _ = x_sample.shape
    zc = jnp.zeros((nbp, CONV_W - 1, LRU_W), F32)
    zl = jnp.zeros((nbp, LRU_W), F32)
    zr = jnp.zeros((nbp, RET_H, RET_DK, RET_DK), F32)
    zh = jnp.zeros((nbp, HG_H, HG_DK, HG_DK), F32)
    yp, cp, lp, rp, hp = _trunk(x_prompt.reshape(nbp * tp, D_MODEL), zc, zl, zr, zh, 0.0, w,
                                nb=nbp, t_len=tp, tm=1024, tn=1024, tb=256, ret_tc=1024, ret_lb=256, ret_hb=8,
                                hg_tc=256, hg_hb=16)
    ys, cs, ls, rs, hs = _trunk(x_sample.reshape(nbs * ts, D_MODEL), state_conv[0], state_lru[0],
                                state_ret[0], state_hgrn[0], float(PAST_LEN), w,
                                nb=nbs, t_len=ts, tm=nbs * ts, tn=1024, tb=ts, ret_tc=ts, ret_lb=ts, ret_hb=8,
                                hg_tc=ts, hg_hb=8)
    return (yp, ys, cp, lp, rp, hp, cs, ls, rs, hs)
```

```python
import functools

import jax
import jax.numpy as jnp
from jax import lax
from jax.experimental import pallas as pl
from jax.experimental.pallas import tpu as pltpu

F32 = jnp.float32
BF16 = jnp.bfloat16

D_MODEL = 2048
CHUNK = 64
D_FF = 5632
EPS = 1e-6
PAST_LEN = 2048
LRU_W = 1024
LRU_BLOCKS = 8
LRU_BS = 128
CONV_W = 4
RG_C = 8.0
RET_H = 8
RET_DK = 128
ROPE_BASE = 10000.0
HG_H = 16
HG_DK = 128

SUBLANES = 8
LANES = 128
VMEM_LIMIT = 56 * 1024 * 1024

NT_DIMS = (((1,), (1,)), ((), ()))
TN_DIMS = (((0,), (0,)), ((), ()))


def _sigmoid(x):
    return 0.5 + 0.5 * jnp.tanh(0.5 * x)


def _silu(x, scale=1.0):
    return (x * (0.5 * scale)) * (1.0 + jnp.tanh(0.5 * x))


def _gelu_tanh(x):
    return 0.5 * x * (1.0 + jnp.tanh(0.7978845608028654 * (x + 0.044715 * (x * x * x))))


def _rms_rows(x, w):
    ms = jnp.mean(x * x, axis=-1, keepdims=True)
    return x * lax.rsqrt(ms + EPS) * w


def _params(n_axes):
    return pltpu.CompilerParams(dimension_semantics=("arbitrary",) * n_axes,
                                vmem_limit_bytes=VMEM_LIMIT)


ROW_STEP = 64


def _norm_rows_to(x_ref, w_ref, h_ref, tm):
    def step(i, _):
        rows = pl.ds(pl.multiple_of(i * ROW_STEP, ROW_STEP), ROW_STEP)
        h_ref[rows, :] = _rms_rows(x_ref[rows, :], w_ref[...]).astype(h_ref.dtype)
        return 0
    lax.fori_loop(0, tm // ROW_STEP, step, 0)


def _ffn_body(x_ref, nw_ref, wg_ref, wu_ref, wd_ref, fnw_ref, o_ref, h_ref, *, tm, nk, final):
    k = pl.program_id(1)

    @pl.when(k == 0)
    def _():
        _norm_rows_to(x_ref, nw_ref, h_ref, tm)
        o_ref[...] = jnp.zeros_like(o_ref)

    h = h_ref[...]
    g = jnp.dot(h, wg_ref[...], preferred_element_type=F32)
    u = jnp.dot(h, wu_ref[...], preferred_element_type=F32)
    a = (_silu(g) * u).astype(BF16)
    o_ref[...] += jnp.dot(a, wd_ref[...], preferred_element_type=F32)

    @pl.when(k == nk - 1)
    def _():
        def step(i, _):
            rows = pl.ds(pl.multiple_of(i * ROW_STEP, ROW_STEP), ROW_STEP)
            y = x_ref[rows, :] + 0.5 * o_ref[rows, :]
            if final:
                y = _rms_rows(y, fnw_ref[...])
            o_ref[rows, :] = y
            return 0
        lax.fori_loop(0, tm // ROW_STEP, step, 0)


def _ffn(x, nw, wg, wu, wd, fnw, *, layer, tm, tf, final):
    m = x.shape[0]
    nk = D_FF // tf
    body = functools.partial(_ffn_body, tm=tm, nk=nk, final=final)
    return pl.pallas_call(
        body,
        grid=(m // tm, nk),
        in_specs=[
            pl.BlockSpec((tm, D_MODEL), lambda i, k: (i, 0)),
            pl.BlockSpec((1, D_MODEL), lambda i, k: (0, 0)),
            pl.BlockSpec((None, D_MODEL, tf), lambda i, k: (layer, 0, k)),
            pl.BlockSpec((None, D_MODEL, tf), lambda i, k: (layer, 0, k)),
            pl.BlockSpec((None, tf, D_MODEL), lambda i, k: (layer, k, 0)),
            pl.BlockSpec((1, D_MODEL), lambda i, k: (0, 0)),
        ],
        out_specs=pl.BlockSpec((tm, D_MODEL), lambda i, k: (i, 0)),
        out_shape=jax.ShapeDtypeStruct((m, D_MODEL), F32),
        scratch_shapes=[pltpu.VMEM((tm, D_MODEL), BF16)],
        compiler_params=_params(2),
        name="ffn",
    )(x, nw, wg, wu, wd, fnw)


def _normproj_body(x_ref, nw_ref, w_ref, o_ref, h_ref, *, tm):
    @pl.when(pl.program_id(1) == 0)
    def _():
        _norm_rows_to(x_ref, nw_ref, h_ref, tm)

    o_ref[...] = jnp.dot(h_ref[...], w_ref[...], preferred_element_type=F32)


def _normproj(x, nw, w, *, tm, tn):
    m = x.shape[0]
    n = w.shape[1]
    return pl.pallas_call(
        functools.partial(_normproj_body, tm=tm),
        grid=(m // tm, n // tn),
        in_specs=[
            pl.BlockSpec((tm, D_MODEL), lambda i, j: (i, 0)),
            pl.BlockSpec((1, D_MODEL), lambda i, j: (0, 0)),
            pl.BlockSpec((D_MODEL, tn), lambda i, j: (0, j)),
        ],
        out_specs=pl.BlockSpec((tm, tn), lambda i, j: (i, j)),
        out_shape=jax.ShapeDtypeStruct((m, n), F32),
        scratch_shapes=[pltpu.VMEM((tm, D_MODEL), BF16)],
        compiler_params=_params(2),
        name="normproj",
    )(x, nw, w)


def _projout_body(*refs, n_in):
    ys = refs[:n_in]
    ws = refs[n_in:2 * n_in]
    res_ref = refs[2 * n_in]
    o_ref = refs[2 * n_in + 1]
    acc = res_ref[...]
    for y_ref, w_ref in zip(ys, ws):
        acc = acc + jnp.dot(y_ref[...], w_ref[...], preferred_element_type=F32)
    o_ref[...] = acc


def _projout(ys, w, res, *, tm):
    m = res.shape[0]
    n_in = len(ys)
    kw = ys[0].shape[1]
    in_specs = [pl.BlockSpec((tm, kw), lambda i: (i, 0)) for _ in ys]
    in_specs += [pl.BlockSpec((kw, D_MODEL), functools.partial(lambda i, j: (j, 0), j=j))
                 for j in range(n_in)]
    in_specs += [pl.BlockSpec((tm, D_MODEL), lambda i: (i, 0))]
    return pl.pallas_call(
        functools.partial(_projout_body, n_in=n_in),
        grid=(m // tm,),
        in_specs=in_specs,
        out_specs=pl.BlockSpec((tm, D_MODEL), lambda i: (i, 0)),
        out_shape=jax.ShapeDtypeStruct((m, D_MODEL), F32),
        compiler_params=_params(1),
        name="projout",
    )(*ys, *([w] * n_in), res)


def _rglru_body(xa_ref, ga_ref, buf0_ref, h0_ref, cw_ref, cb_ref, wa_ref, ba_ref, wx_ref, bx_ref,
                lam_ref, ya_ref, nbuf_ref, nh_ref, ext_ref, a_ref, b_ref, hc_ref, *, tb, nt):
    t = pl.program_id(1)
    halo = SUBLANES

    @pl.when(t == 0)
    def _():
        ext_ref[0:halo, :] = jnp.zeros((halo, LRU_W), F32)
        ext_ref[halo - (CONV_W - 1):halo, :] = buf0_ref[0]
        hc_ref[...] = h0_ref[0]

    @pl.when(t > 0)
    def _():
        ext_ref[0:halo, :] = ext_ref[tb:tb + halo, :]

    x = xa_ref[...]
    ext_ref[halo:halo + tb, :] = x
    xc = ext_ref[halo - 3:halo - 3 + tb, :] * cw_ref[0:1, :]
    xc = xc + ext_ref[halo - 2:halo - 2 + tb, :] * cw_ref[1:2, :]
    xc = xc + ext_ref[halo - 1:halo - 1 + tb, :] * cw_ref[2:3, :]
    xc = xc + x * cw_ref[3:4, :]
    xc = cb_ref[...] + xc

    for n in range(LRU_BLOCKS):
        sl = slice(n * LRU_BS, (n + 1) * LRU_BS)
        xcn = xc[:, sl]
        xb = xcn.astype(BF16)
        r = _sigmoid(jnp.dot(xb, wa_ref[n], preferred_element_type=F32) + ba_ref[:, sl])
        ig = _sigmoid(jnp.dot(xb, wx_ref[n], preferred_element_type=F32) + bx_ref[:, sl])
        nlam = -lam_ref[:, sl]
        softplus = jnp.maximum(nlam, 0.0) + jnp.log(1.0 + jnp.exp(-jnp.abs(nlam)))
        log_a = (-RG_C) * r * softplus
        a = jnp.exp(log_a)
        mult = jnp.sqrt(-jnp.tanh(log_a) * (a * a + 1.0))
        a_ref[:, sl] = a
        b_ref[:, sl] = mult * (ig * xcn)

    rowi = lax.broadcasted_iota(jnp.int32, (SUBLANES, LRU_W), 0)

    def group(gi, carry):
        rows = pl.ds(pl.multiple_of(gi * SUBLANES, SUBLANES), SUBLANES)
        av = a_ref[rows, :]
        bv = b_ref[rows, :]
        for s in (1, 2, 4):
            m = rowi >= s
            ash = pltpu.roll(av, s, 0)
            bsh = pltpu.roll(bv, s, 0)
            bv = jnp.where(m, av * bsh + bv, bv)
            av = jnp.where(m, av * ash, av)
        hrows = av * carry + bv
        b_ref[rows, :] = hrows
        return hrows[SUBLANES - 1:SUBLANES, :]

    carry = lax.fori_loop(0, tb // SUBLANES, group, hc_ref[...], unroll=4)
    hc_ref[...] = carry

    ya_ref[...] = (b_ref[...] * _gelu_tanh(ga_ref[...])).astype(ya_ref.dtype)

    @pl.when(t == nt - 1)
    def _():
        nbuf_ref[0] = ext_ref[halo + tb - (CONV_W - 1):halo + tb, :]
        nh_ref[0] = carry


def _rglru(z, buf0, h0, cw, cb, wa, ba, wx, bx, lam, *, nb, t_len, tb):
    nt = t_len // tb
    row1 = lambda b, t: (0, 0)
    return pl.pallas_call(
        functools.partial(_rglru_body, tb=tb, nt=nt),
        grid=(nb, nt),
        in_specs=[
            pl.BlockSpec((tb, LRU_W), lambda b, t: (b * nt + t, 0)),
            pl.BlockSpec((tb, LRU_W), lambda b, t: (b * nt + t, 1)),
            pl.BlockSpec((1, CONV_W - 1, LRU_W), lambda b, t: (b, 0, 0)),
            pl.BlockSpec((1, 1, LRU_W), lambda b, t: (b, 0, 0)),
            pl.BlockSpec((CONV_W, LRU_W), row1),
            pl.BlockSpec((1, LRU_W), row1),
            pl.BlockSpec((LRU_BLOCKS, LRU_BS, LRU_BS), lambda b, t: (0, 0, 0)),
            pl.BlockSpec((1, LRU_W), row1),
            pl.BlockSpec((LRU_BLOCKS, LRU_BS, LRU_BS), lambda b, t: (0, 0, 0)),
            pl.BlockSpec((1, LRU_W), row1),
            pl.BlockSpec((1, LRU_W), row1),
        ],
        out_specs=[
            pl.BlockSpec((tb, LRU_W), lambda b, t: (b * nt + t, 0)),
            pl.BlockSpec((1, CONV_W - 1, LRU_W), lambda b, t: (b, 0, 0)),
            pl.BlockSpec((1, 1, LRU_W), lambda b, t: (b, 0, 0)),
        ],
        out_shape=[
            jax.ShapeDtypeStruct((nb * t_len, LRU_W), BF16),
            jax.ShapeDtypeStruct((nb, CONV_W - 1, LRU_W), F32),
            jax.ShapeDtypeStruct((nb, 1, LRU_W), F32),
        ],
        scratch_shapes=[
            pltpu.VMEM((tb + SUBLANES, LRU_W), F32),
            pltpu.VMEM((tb, LRU_W), F32),
            pltpu.VMEM((tb, LRU_W), F32),
            pltpu.VMEM((1, LRU_W), F32),
        ],
        compiler_params=_params(2),
        name="rglru",
    )(z, z, buf0, h0, cw, cb, wa, ba, wx, bx, lam)


def _retention_body(q_ref, k_ref, v_ref, g_ref, cos_ref, sin_ref, s0_ref, lg_ref, nw_ref,
                    y_ref, so_ref, s_ref, *, lb, nsub, nt, chunk, hb):
    t = pl.program_id(2)

    @pl.when(t == 0)
    def _():
        s_ref[...] = s0_ref[0]

    ti = lax.broadcasted_iota(jnp.int32, (lb, lb), 0)
    si = lax.broadcasted_iota(jnp.int32, (lb, lb), 1)
    dt = (ti - si).astype(F32)
    tc = ti & (-chunk)
    sc = si & (-chunk)
    same = tc == sc
    earlier = sc < tc
    adt = jnp.abs(dt)
    pdt = jnp.maximum(dt, 0.0)
    rowf = lax.broadcasted_iota(jnp.int32, (lb, RET_DK), 0).astype(F32)
    scale = RET_DK ** -0.5

    consts = []
    for h in range(hb):
        lg = lg_ref[h][:, 0:1]
        dmask = jnp.where(same, jnp.exp(lg * adt), jnp.where(earlier, jnp.exp(lg * pdt), 0.0))
        qdec = jnp.exp(lg * (rowf + 1.0))
        kdec = jnp.exp(lg * ((lb - 1.0) - rowf))
        cdec = jnp.exp(lg * float(lb))
        consts.append((dmask, qdec, kdec, cdec))

    def sub(j, _):
        rows = pl.ds(pl.multiple_of(j * lb, lb), lb)
        cs = cos_ref[rows, :]
        sn = sin_ref[rows, :]
        staged = []
        for h in range(hb):
            dmask, qdec, kdec, cdec = consts[h]
            cols = slice(h * RET_DK, (h + 1) * RET_DK)
            q = q_ref[rows, cols]
            k = k_ref[rows, cols]
            v = v_ref[rows, cols].astype(BF16)
            qr = (q * cs + pltpu.roll(q, RET_DK // 2, 1) * sn) * scale
            kr = k * cs + pltpu.roll(k, RET_DK // 2, 1) * sn
            att = lax.dot_general(qr.astype(BF16), kr.astype(BF16), NT_DIMS,
                                  preferred_element_type=F32) * dmask
            staged.append((att.astype(BF16), (qr * qdec).astype(BF16), (kr * kdec).astype(BF16), v))
        for h in range(hb):
            att, qd, kd, v = staged[h]
            cdec = consts[h][3]
            cols = slice(h * RET_DK, (h + 1) * RET_DK)
            s = s_ref[h]
            o = jnp.dot(att, v, preferred_element_type=F32)
            o = o + jnp.dot(qd, s.astype(BF16), preferred_element_type=F32)
            s_ref[h] = cdec * s + lax.dot_general(kd, v, TN_DIMS, preferred_element_type=F32)
            yn = o * lax.rsqrt(jnp.mean(o * o, axis=-1, keepdims=True) + EPS)
            y_ref[rows, cols] = (yn * nw_ref[:, cols] * _silu(g_ref[rows, cols])).astype(y_ref.dtype)
        return 0

    lax.fori_loop(0, nsub, sub, 0)

    @pl.when(t == nt - 1)
    def _():
        so_ref[0] = s_ref[...]


def _retention(z, cosf, sinf, s0, lgt, nw, *, nb, t_len, tc, lb, chunk, shared_pos, hb):
    nt = t_len // tc
    wb = hb * RET_DK
    col0 = 2 * LRU_W // wb
    ngrp = RET_H // hb
    pos_map = (lambda b, h, t: (0, 0)) if shared_pos else (lambda b, h, t: (b * nt + t, 0))

    def zspec(off):
        return pl.BlockSpec((tc, wb), lambda b, h, t: (b * nt + t, col0 + off * ngrp + h))

    return pl.pallas_call(
        functools.partial(_retention_body, lb=lb, nsub=tc // lb, nt=nt, chunk=chunk, hb=hb),
        grid=(nb, ngrp, nt),
        in_specs=[
            zspec(0), zspec(1), zspec(2), zspec(3),
            pl.BlockSpec((tc, RET_DK), pos_map),
            pl.BlockSpec((tc, RET_DK), pos_map),
            pl.BlockSpec((1, hb, RET_DK, RET_DK), lambda b, h, t: (b, h, 0, 0)),
            pl.BlockSpec((hb, 1, RET_DK), lambda b, h, t: (h, 0, 0)),
            pl.BlockSpec((1, wb), lambda b, h, t: (0, h)),
        ],
        out_specs=[
            pl.BlockSpec((tc, wb), lambda b, h, t: (b * nt + t, h)),
            pl.BlockSpec((1, hb, RET_DK, RET_DK), lambda b, h, t: (b, h, 0, 0)),
        ],
        out_shape=[
            jax.ShapeDtypeStruct((nb * t_len, RET_H * RET_DK), BF16),
            jax.ShapeDtypeStruct((nb, RET_H, RET_DK, RET_DK), F32),
        ],
        scratch_shapes=[pltpu.VMEM((hb, RET_DK, RET_DK), F32)],
        compiler_params=_params(3),
        name="retention",
    )(z, z, z, z, cosf, sinf, s0, lgt, nw)


def _hgrn_body(q_ref, f_ref, v_ref, g_ref, lbl_ref, nw_ref, s0_ref, y_ref, so_ref, st_ref,
               *, chunk, nsub, nt, layer, hb):
    t = pl.program_id(2)
    L = chunk

    @pl.when(t == 0)
    def _():
        for h in range(hb):
            st_ref[h] = s0_ref[0, h].T

    lgt = lbl_ref[...]
    mx = jnp.max(lgt, axis=0, keepdims=True)
    ex = jnp.exp(lgt - mx)
    p = ex / jnp.sum(ex, axis=0, keepdims=True)
    lb_all = jnp.sum(p[0:layer + 1, :], axis=0, keepdims=True) - p[0:1, :]

    rowi = lax.broadcasted_iota(jnp.int32, (L, HG_DK), 0)
    ti = lax.broadcasted_iota(jnp.int32, (L, L), 0)
    si = lax.broadcasted_iota(jnp.int32, (L, L), 1)
    txs = ti ^ si
    below = si < ti
    eye = ti == si
    level_mask = {}
    blk = L
    while blk >= 2:
        level_mask[blk] = below & (txs >= blk // 2) & (txs < blk)
        blk //= 2
    upper = {}
    sign = {}
    blk = L
    while blk >= SUBLANES:
        upper[blk] = (rowi & (blk - 1)) >= blk // 2
        sign[blk] = jnp.where(upper[blk], 1.0, -1.0)
        blk //= 2
    i4 = rowi & 3
    odd = (rowi & 1) == 1
    row8 = lax.broadcasted_iota(jnp.int32, (SUBLANES, HG_DK), 0)
    nreg = L // SUBLANES
    scale = HG_DK ** -0.5

    def gram(x):
        xb = x.astype(BF16)
        return lax.dot_general(xb, xb, NT_DIMS, preferred_element_type=F32)

    def per_reg(x, fn):
        return jnp.concatenate([fn(x[i * SUBLANES:(i + 1) * SUBLANES, :]) for i in range(nreg)], axis=0)

    def running_sum(x):
        parts = [x[i * SUBLANES:(i + 1) * SUBLANES, :] for i in range(nreg)]
        for s in (1, 2, 4):
            parts = [p + jnp.where(row8 >= s, pltpu.roll(p, s, 0), 0.0) for p in parts]
        out = []
        carry = None
        for p in parts:
            if carry is not None:
                p = p + carry
            out.append(p)
            carry = p[SUBLANES - 1:SUBLANES, :]
        return jnp.concatenate(out, axis=0)

    def one_head(rows, h):
        cols = slice(h * HG_DK, (h + 1) * HG_DK)
        lbv = lb_all[:, cols]
        f = lbv + (1.0 - lbv) * _sigmoid(f_ref[rows, cols])
        qh = _silu(q_ref[rows, cols], scale)
        kk = 1.0 - f
        v = v_ref[rows, cols].astype(BF16)
        c = running_sum(jnp.log2(f))
        clast = c[L - 1:L, :]

        att = jnp.where(eye, jnp.sum(qh * kk, axis=-1, keepdims=True), 0.0)
        blk = L
        while blk >= SUBLANES:
            half = blk // 2
            r = jnp.concatenate(
                [jnp.broadcast_to(c[i * blk + half - 1:i * blk + half, :], (blk, HG_DK))
                 for i in range(L // blk)], axis=0)
            x = jnp.where(upper[blk], qh, kk) * jnp.exp2((c - r) * sign[blk])
            att = jnp.where(level_mask[blk], gram(x), att)
            blk = half
        fprev = per_reg(f, lambda p: pltpu.roll(p, 1, 0))
        fnext = per_reg(f, lambda p: pltpu.roll(p, SUBLANES - 1, 0))
        qf = qh * f
        x4 = jnp.where(i4 == 0, kk * fnext, jnp.where(i4 == 1, kk, jnp.where(i4 == 2, qf, qf * fprev)))
        att = jnp.where(level_mask[4], gram(x4), att)
        att = jnp.where(level_mask[2], gram(jnp.where(odd, qf, kk)), att)
        qe = (qh * jnp.exp2(c)).astype(BF16)
        kd = (kk * jnp.exp2(clast - c)).astype(BF16)
        return att.astype(BF16), qe, kd, v, jnp.exp2(clast)

    def head_state(rows, h, att, qe, kd, v, dec):
        cols = slice(h * HG_DK, (h + 1) * HG_DK)
        st = st_ref[h]
        o = jnp.dot(att, v, preferred_element_type=F32)
        o = o + lax.dot_general(qe, st.astype(BF16), NT_DIMS, preferred_element_type=F32)
        st_ref[h] = dec * st + lax.dot_general(v, kd, TN_DIMS, preferred_element_type=F32)
        yn = o * lax.rsqrt(jnp.mean(o * o, axis=-1, keepdims=True) + EPS)
        y_ref[rows, cols] = (yn * nw_ref[:, cols] * _silu(g_ref[rows, cols])).astype(y_ref.dtype)

    def sub(j, _):
        rows = pl.ds(pl.multiple_of(j * L, L), L)
        staged = [one_head(rows, h) for h in range(hb)]
        for h in range(hb):
            head_state(rows, h, *staged[h])
        return 0

    lax.fori_loop(0, nsub, sub, 0)

    @pl.when(t == nt - 1)
    def _():
        for h in range(hb):
            so_ref[0, h] = st_ref[h].T


def _hgrn(z, logits, nw, s0, *, nb, t_len, tc, chunk, layer, hb):
    nt = t_len // tc
    wb = hb * HG_DK
    ngrp = HG_H // hb

    def zspec(off):
        return pl.BlockSpec((tc, wb), lambda b, h, t: (b * nt + t, off * ngrp + h))

    return pl.pallas_call(
        functools.partial(_hgrn_body, chunk=chunk, nsub=tc // chunk, nt=nt, layer=layer, hb=hb),
        grid=(nb, ngrp, nt),
        in_specs=[
            zspec(0), zspec(1), zspec(2), zspec(3),
            pl.BlockSpec((logits.shape[0], wb), lambda b, h, t: (0, h)),
            pl.BlockSpec((1, wb), lambda b, h, t: (0, h)),
            pl.BlockSpec((1, hb, HG_DK, HG_DK), lambda b, h, t: (b, h, 0, 0)),
        ],
        out_specs=[
            pl.BlockSpec((tc, wb), lambda b, h, t: (b * nt + t, h)),
            pl.BlockSpec((1, hb, HG_DK, HG_DK), lambda b, h, t: (b, h, 0, 0)),
        ],
        out_shape=[
            jax.ShapeDtypeStruct((nb * t_len, HG_H * HG_DK), BF16),
            jax.ShapeDtypeStruct((nb, HG_H, HG_DK, HG_DK), F32),
        ],
        scratch_shapes=[pltpu.VMEM((hb, HG_DK, HG_DK), F32)],
        compiler_params=_params(3),
        name="hgrn",
    )(z, z, z, z, logits, nw, s0)


def _rope_tables(pos):
    half = RET_DK // 2
    inv = ROPE_BASE ** (-jnp.arange(half, dtype=F32) / half)
    ang = pos[:, None] * inv[None, :]
    cos = jnp.cos(ang)
    sin = jnp.sin(ang)
    return jnp.concatenate([cos, cos], axis=-1), jnp.concatenate([-sin, sin], axis=-1)


def _trunk(x, conv0, lru0, ret0, hg0, pos0, w, *, nb, t_len, tm, tn, tb, ret_tc, ret_lb, ret_hb,
           hg_tc, hg_hb):
    chunk = min(t_len, CHUNK)
    cosf, sinf = _rope_tables(pos0 + jnp.arange(t_len, dtype=F32))
    log_g = jnp.log1p(-jnp.exp2(-5.0 - jnp.arange(RET_H, dtype=F32)))
    lgt = jnp.broadcast_to(log_g[:, None, None], (RET_H, 1, RET_DK))
    row = lambda a: a.reshape(1, -1)

    x = _ffn(x, row(w["ffn1_norm"][0]), w["ffn1_wg"], w["ffn1_wu"], w["ffn1_wd"],
             row(w["final_norm"]), layer=0, tm=tm, tf=512, final=False)
    z = _normproj(x, row(w["mix_norm"][0]), w["ab_w_in"][0], tm=tm, tn=tn)
    ya, nbuf, nh = _rglru(z, conv0, lru0.reshape(nb, 1, LRU_W), w["ab_conv_w"][0], row(w["ab_conv_b"][0]),
                          w["ab_gate_a_w"][0], row(w["ab_gate_a_b"][0]), w["ab_gate_x_w"][0],
                          row(w["ab_gate_x_b"][0]), row(w["ab_lru_lambda"][0]),
                          nb=nb, t_len=t_len, tb=tb)
    yr, ns = _retention(z, cosf, sinf, ret0, lgt, row(w["ab_ret_norm"][0]), nb=nb, t_len=t_len,
                        tc=ret_tc, lb=ret_lb, chunk=chunk, shared_pos=nb > 1, hb=ret_hb)
    x = _projout([ya, yr], w["ab_w_out"][0], x, tm=min(tm, 512))
    x = _ffn(x, row(w["ffn2_norm"][0]), w["ffn2_wg"], w["ffn2_wu"], w["ffn2_wd"],
             row(w["final_norm"]), layer=0, tm=tm, tf=512, final=False)

    x = _ffn(x, row(w["ffn1_norm"][1]), w["ffn1_wg"], w["ffn1_wu"], w["ffn1_wd"],
             row(w["final_norm"]), layer=1, tm=tm, tf=512, final=False)
    z = _normproj(x, row(w["mix_norm"][1]), w["c_w_in"][0], tm=tm, tn=tn)
    yc, nhg = _hgrn(z, w["c_lb_logits"], row(w["c_norm"][0]), hg0, nb=nb, t_len=t_len, tc=hg_tc,
                    chunk=chunk, layer=1, hb=hg_hb)
    x = _projout([yc], w["c_w_out"][0], x, tm=min(tm, 512))
    y = _ffn(x, row(w["ffn2_norm"][1]), w["ffn2_wg"], w["ffn2_wu"], w["ffn2_wd"],
             row(w["final_norm"]), layer=1, tm=tm, tf=512, final=True)
    return (y.reshape(nb, t_len, D_MODEL), nbuf[None], nh.reshape(1, nb, LRU_W), ns[None], nhg[None])


_MATMUL_WEIGHTS = ("ffn1_wg", "ffn1_wu", "ffn1_wd", "ffn2_wg", "ffn2_wu", "ffn2_wd",
                   "ab_w_in", "ab_gate_a_w", "ab_gate_x_w", "ab_w_out", "c_w_in", "c_w_out")


def kernel(x_prompt, x_sample, state_conv, state_lru, state_ret, state_hgrn, ffn1_norm, ffn1_wg, ffn1_wu, ffn1_wd, mix_norm, ffn2_norm, ffn2_wg, ffn2_wu, ffn2_wd, final_norm, ab_w_in, ab_conv_w, ab_conv_b, ab_gate_a_w, ab_gate_a_b, ab_gate_x_w, ab_gate_x_b, ab_lru_lambda, ab_ret_norm, ab_w_out, c_w_in, c_lb_logits, c_norm, c_w_out):
    w = dict(ffn1_norm=ffn1_norm, ffn1_wg=ffn1_wg, ffn1_wu=ffn1_wu, ffn1_wd=ffn1_wd, mix_norm=mix_norm,
             ffn2_norm=ffn2_norm, ffn2_wg=ffn2_wg, ffn2_wu=ffn2_wu, ffn2_wd=ffn2_wd,
             final_norm=final_norm, ab_w_in=ab_w_in, ab_conv_w=ab_conv_w, ab_conv_b=ab_conv_b,
             ab_gate_a_w=ab_gate_a_w, ab_gate_a_b=ab_gate_a_b, ab_gate_x_w=ab_gate_x_w,
             ab_gate_x_b=ab_gate_x_b, ab_lru_lambda=ab_lru_lambda, ab_ret_norm=ab_ret_norm,
             ab_w_out=ab_w_out, c_w_in=c_w_in, c_lb_logits=c_lb_logits, c_norm=c_norm, c_w_out=c_w_out)
    for name in _MATMUL_WEIGHTS:
        w[name] = w[name].astype(BF16)

    nbp, tp, _ = x_prompt.shape
    nbs, ts, _ = x_sample.shape
    zc = jnp.zeros((nbp, CONV_W - 1, LRU_W), F32)
    zl = jnp.zeros((nbp, LRU_W), F32)
    zr = jnp.zeros((nbp, RET_H, RET_DK, RET_DK), F32)
    zh = jnp.zeros((nbp, HG_H, HG_DK, HG_DK), F32)
    yp, cp, lp, rp, hp = _trunk(x_prompt.reshape(nbp * tp, D_MODEL), zc, zl, zr, zh, 0.0, w,
                                nb=nbp, t_len=tp, tm=1024, tn=1024, tb=256, ret_tc=1024, ret_lb=256, ret_hb=8,
                                hg_tc=256, hg_hb=16)
    ys, cs, ls, rs, hs = _trunk(x_sample.reshape(nbs * ts, D_MODEL), state_conv[0], state_lru[0],
                                state_ret[0], state_hgrn[0], float(PAST_LEN), w,
                                nb=nbs, t_len=ts, tm=nbs * ts, tn=1024, tb=ts, ret_tc=ts, ret_lb=ts, ret_hb=8,
                                hg_tc=ts, hg_hb=8)
    return (yp, ys, cp, lp, rp, hp, cs, ls, rs, hs)
```

```python
import functools

import jax
import jax.numpy as jnp
from jax import lax
from jax.experimental import pallas as pl
from jax.experimental.pallas import tpu as pltpu

F32 = jnp.float32
BF16 = jnp.bfloat16

D_MODEL = 2048
CHUNK = 64
D_FF = 5632
EPS = 1e-6
PAST_LEN = 2048
LRU_W = 1024
LRU_BLOCKS = 8
LRU_BS = 128
CONV_W = 4
RG_C = 8.0
RET_H = 8
RET_DK = 128
ROPE_BASE = 10000.0
HG_H = 16
HG_DK = 128

SUBLANES = 8
LANES = 128
VMEM_LIMIT = 56 * 1024 * 1024

NT_DIMS = (((1,), (1,)), ((), ()))
TN_DIMS = (((0,), (0,)), ((), ()))


def _sigmoid(x):
    return 0.5 + 0.5 * jnp.tanh(0.5 * x)


def _silu(x, scale=1.0):
    return (x * (0.5 * scale)) * (1.0 + jnp.tanh(0.5 * x))


def _gelu_tanh(x):
    return 0.5 * x * (1.0 + jnp.tanh(0.7978845608028654 * (x + 0.044715 * (x * x * x))))


def _rms_rows(x, w):
    ms = jnp.mean(x * x, axis=-1, keepdims=True)
    return x * lax.rsqrt(ms + EPS) * w


def _params(n_axes):
    return pltpu.CompilerParams(dimension_semantics=("arbitrary",) * n_axes,
                                vmem_limit_bytes=VMEM_LIMIT)


ROW_STEP = 64


def _norm_rows_to(x_ref, w_ref, h_ref, tm):
    def step(i, _):
        rows = pl.ds(pl.multiple_of(i * ROW_STEP, ROW_STEP), ROW_STEP)
        h_ref[rows, :] = _rms_rows(x_ref[rows, :], w_ref[...]).astype(h_ref.dtype)
        return 0
    lax.fori_loop(0, tm // ROW_STEP, step, 0)


def _ffn_body(x_ref, nw_ref, wg_ref, wu_ref, wd_ref, fnw_ref, o_ref, h_ref, *, tm, nk, final):
    k = pl.program_id(1)

    @pl.when(k == 0)
    def _():
        _norm_rows_to(x_ref, nw_ref, h_ref, tm)
        o_ref[...] = jnp.zeros_like(o_ref)

    h = h_ref[...]
    g = jnp.dot(h, wg_ref[...], preferred_element_type=F32)
    u = jnp.dot(h, wu_ref[...], preferred_element_type=F32)
    a = (_silu(g) * u).astype(BF16)
    o_ref[...] += jnp.dot(a, wd_ref[...], preferred_element_type=F32)

    @pl.when(k == nk - 1)
    def _():
        def step(i, _):
            rows = pl.ds(pl.multiple_of(i * ROW_STEP, ROW_STEP), ROW_STEP)
            y = x_ref[rows, :] + 0.5 * o_ref[rows, :]
            if final:
                y = _rms_rows(y, fnw_ref[...])
            o_ref[rows, :] = y
            return 0
        lax.fori_loop(0, tm // ROW_STEP, step, 0)


def _ffn(x, nw, wg, wu, wd, fnw, *, layer, tm, tf, final):
    m = x.shape[0]
    nk = D_FF // tf
    body = functools.partial(_ffn_body, tm=tm, nk=nk, final=final)
    return pl.pallas_call(
        body,
        grid=(m // tm, nk),
        in_specs=[
            pl.BlockSpec((tm, D_MODEL), lambda i, k: (i, 0)),
            pl.BlockSpec((1, D_MODEL), lambda i, k: (0, 0)),
            pl.BlockSpec((None, D_MODEL, tf), lambda i, k: (layer, 0, k)),
            pl.BlockSpec((None, D_MODEL, tf), lambda i, k: (layer, 0, k)),
            pl.BlockSpec((None, tf, D_MODEL), lambda i, k: (layer, k, 0)),
            pl.BlockSpec((1, D_MODEL), lambda i, k: (0, 0)),
        ],
        out_specs=pl.BlockSpec((tm, D_MODEL), lambda i, k: (i, 0)),
        out_shape=jax.ShapeDtypeStruct((m, D_MODEL), F32),
        scratch_shapes=[pltpu.VMEM((tm, D_MODEL), BF16)],
        compiler_params=_params(2),
        name="ffn",
    )(x, nw, wg, wu, wd, fnw)


def _normproj_body(x_ref, nw_ref, w_ref, o_ref, h_ref, *, tm):
    @pl.when(pl.program_id(1) == 0)
    def _():
        _norm_rows_to(x_ref, nw_ref, h_ref, tm)

    o_ref[...] = jnp.dot(h_ref[...], w_ref[...], preferred_element_type=F32)


def _normproj(x, nw, w, *, tm, tn):
    m = x.shape[0]
    n = w.shape[1]
    return pl.pallas_call(
        functools.partial(_normproj_body, tm=tm),
        grid=(m // tm, n // tn),
        in_specs=[
            pl.BlockSpec((tm, D_MODEL), lambda i, j: (i, 0)),
            pl.BlockSpec((1, D_MODEL), lambda i, j: (0, 0)),
            pl.BlockSpec((D_MODEL, tn), lambda i, j: (0, j)),
        ],
        out_specs=pl.BlockSpec((tm, tn), lambda i, j: (i, j)),
        out_shape=jax.ShapeDtypeStruct((m, n), F32),
        scratch_shapes=[pltpu.VMEM((tm, D_MODEL), BF16)],
        compiler_params=_params(2),
        name="normproj",
    )(x, nw, w)


def _projout_body(*refs, n_in):
    ys = refs[:n_in]
    ws = refs[n_in:2 * n_in]
    res_ref = refs[2 * n_in]
    o_ref = refs[2 * n_in + 1]
    acc = res_ref[...]
    for y_ref, w_ref in zip(ys, ws):
        acc = acc + jnp.dot(y_ref[...], w_ref[...], preferred_element_type=F32)
    o_ref[...] = acc


def _projout(ys, w, res, *, tm):
    m = res.shape[0]
    n_in = len(ys)
    kw = ys[0].shape[1]
    in_specs = [pl.BlockSpec((tm, kw), lambda i: (i, 0)) for _ in ys]
    in_specs += [pl.BlockSpec((kw, D_MODEL), functools.partial(lambda i, j: (j, 0), j=j))
                 for j in range(n_in)]
    in_specs += [pl.BlockSpec((tm, D_MODEL), lambda i: (i, 0))]
    return pl.pallas_call(
        functools.partial(_projout_body, n_in=n_in),
        grid=(m // tm,),
        in_specs=in_specs,
        out_specs=pl.BlockSpec((tm, D_MODEL), lambda i: (i, 0)),
        out_shape=jax.ShapeDtypeStruct((m, D_MODEL), F32),
        compiler_params=_params(1),
        name="projout",
    )(*ys, *([w] * n_in), res)


def _rglru_body(xa_ref, ga_ref, buf0_ref, h0_ref, cw_ref, cb_ref, wa_ref, ba_ref, wx_ref, bx_ref,
                lam_ref, ya_ref, nbuf_ref, nh_ref, ext_ref, a_ref, b_ref, hc_ref, *, tb, nt):
    t = pl.program_id(1)
    halo = SUBLANES

    @pl.when(t == 0)
    def _():
        ext_ref[0:halo, :] = jnp.zeros((halo, LRU_W), F32)
        ext_ref[halo - (CONV_W - 1):halo, :] = buf0_ref[0]
        hc_ref[...] = h0_ref[0]

    @pl.when(t > 0)
    def _():
        ext_ref[0:halo, :] = ext_ref[tb:tb + halo, :]

    x = xa_ref[...]
    ext_ref[halo:halo + tb, :] = x
    xc = ext_ref[halo - 3:halo - 3 + tb, :] * cw_ref[0:1, :]
    xc = xc + ext_ref[halo - 2:halo - 2 + tb, :] * cw_ref[1:2, :]
    xc = xc + ext_ref[halo - 1:halo - 1 + tb, :] * cw_ref[2:3, :]
    xc = xc + x * cw_ref[3:4, :]
    xc = cb_ref[...] + xc

    for n in range(LRU_BLOCKS):
        sl = slice(n * LRU_BS, (n + 1) * LRU_BS)
        xcn = xc[:, sl]
        xb = xcn.astype(BF16)
        r = _sigmoid(jnp.dot(xb, wa_ref[n], preferred_element_type=F32) + ba_ref[:, sl])
        ig = _sigmoid(jnp.dot(xb, wx_ref[n], preferred_element_type=F32) + bx_ref[:, sl])
        nlam = -lam_ref[:, sl]
        softplus = jnp.maximum(nlam, 0.0) + jnp.log(1.0 + jnp.exp(-jnp.abs(nlam)))
        log_a = (-RG_C) * r * softplus
        a = jnp.exp(log_a)
        mult = jnp.sqrt(-jnp.tanh(log_a) * (a * a + 1.0))
        a_ref[:, sl] = a
        b_ref[:, sl] = mult * (ig * xcn)

    rowi = lax.broadcasted_iota(jnp.int32, (SUBLANES, LRU_W), 0)

    def group(gi, carry):
        rows = pl.ds(pl.multiple_of(gi * SUBLANES, SUBLANES), SUBLANES)
        av = a_ref[rows, :]
        bv = b_ref[rows, :]
        for s in (1, 2, 4):
            m = rowi >= s
            ash = pltpu.roll(av, s, 0)
            bsh = pltpu.roll(bv, s, 0)
            bv = jnp.where(m, av * bsh + bv, bv)
            av = jnp.where(m, av * ash, av)
        hrows = av * carry + bv
        b_ref[rows, :] = hrows
        return hrows[SUBLANES - 1:SUBLANES, :]

    carry = lax.fori_loop(0, tb // SUBLANES, group, hc_ref[...], unroll=4)
    hc_ref[...] = carry

    ya_ref[...] = (b_ref[...] * _gelu_tanh(ga_ref[...])).astype(ya_ref.dtype)

    @pl.when(t == nt - 1)
    def _():
        nbuf_ref[0] = ext_ref[halo + tb - (CONV_W - 1):halo + tb, :]
        nh_ref[0] = carry


def _rglru(z, buf0, h0, cw, cb, wa, ba, wx, bx, lam, *, nb, t_len, tb):
    nt = t_len // tb
    row1 = lambda b, t: (0, 0)
    return pl.pallas_call(
        functools.partial(_rglru_body, tb=tb, nt=nt),
        grid=(nb, nt),
        in_specs=[
            pl.BlockSpec((tb, LRU_W), lambda b, t: (b * nt + t, 0)),
            pl.BlockSpec((tb, LRU_W), lambda b, t: (b * nt + t, 1)),
            pl.BlockSpec((1, CONV_W - 1, LRU_W), lambda b, t: (b, 0, 0)),
            pl.BlockSpec((1, 1, LRU_W), lambda b, t: (b, 0, 0)),
            pl.BlockSpec((CONV_W, LRU_W), row1),
            pl.BlockSpec((1, LRU_W), row1),
            pl.BlockSpec((LRU_BLOCKS, LRU_BS, LRU_BS), lambda b, t: (0, 0, 0)),
            pl.BlockSpec((1, LRU_W), row1),
            pl.BlockSpec((LRU_BLOCKS, LRU_BS, LRU_BS), lambda b, t: (0, 0, 0)),
            pl.BlockSpec((1, LRU_W), row1),
            pl.BlockSpec((1, LRU_W), row1),
        ],
        out_specs=[
            pl.BlockSpec((tb, LRU_W), lambda b, t: (b * nt + t, 0)),
            pl.BlockSpec((1, CONV_W - 1, LRU_W), lambda b, t: (b, 0, 0)),
            pl.BlockSpec((1, 1, LRU_W), lambda b, t: (b, 0, 0)),
        ],
        out_shape=[
            jax.ShapeDtypeStruct((nb * t_len, LRU_W), BF16),
            jax.ShapeDtypeStruct((nb, CONV_W - 1, LRU_W), F32),
            jax.ShapeDtypeStruct((nb, 1, LRU_W), F32),
        ],
        scratch_shapes=[
            pltpu.VMEM((tb + SUBLANES, LRU_W), F32),
            pltpu.VMEM((tb, LRU_W), F32),
            pltpu.VMEM((tb, LRU_W), F32),
            pltpu.VMEM((1, LRU_W), F32),
        ],
        compiler_params=_params(2),
        name="rglru",
    )(z, z, buf0, h0, cw, cb, wa, ba, wx, bx, lam)


def _retention_body(q_ref, k_ref, v_ref, g_ref, cos_ref, sin_ref, s0_ref, lg_ref, nw_ref,
                    y_ref, so_ref, s_ref, *, lb, nsub, nt, chunk, hb):
    t = pl.program_id(2)

    @pl.when(t == 0)
    def _():
        s_ref[...] = s0_ref[0]

    ti = lax.broadcasted_iota(jnp.int32, (lb, lb), 0)
    si = lax.broadcasted_iota(jnp.int32, (lb, lb), 1)
    dt = (ti - si).astype(F32)
    tc = ti & (-chunk)
    sc = si & (-chunk)
    same = tc == sc
    earlier = sc < tc
    adt = jnp.abs(dt)
    pdt = jnp.maximum(dt, 0.0)
    rowf = lax.broadcasted_iota(jnp.int32, (lb, RET_DK), 0).astype(F32)
    scale = RET_DK ** -0.5

    consts = []
    for h in range(hb):
        lg = lg_ref[h][:, 0:1]
        dmask = jnp.where(same, jnp.exp(lg * adt), jnp.where(earlier, jnp.exp(lg * pdt), 0.0))
        qdec = jnp.exp(lg * (rowf + 1.0))
        kdec = jnp.exp(lg * ((lb - 1.0) - rowf))
        cdec = jnp.exp(lg * float(lb))
        consts.append((dmask, qdec, kdec, cdec))

    def sub(j, _):
        rows = pl.ds(pl.multiple_of(j * lb, lb), lb)
        cs = cos_ref[rows, :]
        sn = sin_ref[rows, :]
        staged = []
        for h in range(hb):
            dmask, qdec, kdec, cdec = consts[h]
            cols = slice(h * RET_DK, (h + 1) * RET_DK)
            q = q_ref[rows, cols]
            k = k_ref[rows, cols]
            v = v_ref[rows, cols].astype(BF16)
            qr = (q * cs + pltpu.roll(q, RET_DK // 2, 1) * sn) * scale
            kr = k * cs + pltpu.roll(k, RET_DK // 2, 1) * sn
            att = lax.dot_general(qr.astype(BF16), kr.astype(BF16), NT_DIMS,
                                  preferred_element_type=F32) * dmask
            staged.append((att.astype(BF16), (qr * qdec).astype(BF16), (kr * kdec).astype(BF16), v))
        for h in range(hb):
            att, qd, kd, v = staged[h]
            cdec = consts[h][3]
            cols = slice(h * RET_DK, (h + 1) * RET_DK)
            s = s_ref[h]
            o = jnp.dot(att, v, preferred_element_type=F32)
            o = o + jnp.dot(qd, s.astype(BF16), preferred_element_type=F32)
            s_ref[h] = cdec * s + lax.dot_general(kd, v, TN_DIMS, preferred_element_type=F32)
            yn = o * lax.rsqrt(jnp.mean(o * o, axis=-1, keepdims=True) + EPS)
            y_ref[rows, cols] = (yn * nw_ref[:, cols] * _silu(g_ref[rows, cols])).astype(y_ref.dtype)
        return 0

    lax.fori_loop(0, nsub, sub, 0)

    @pl.when(t == nt - 1)
    def _():
        so_ref[0] = s_ref[...]


def _retention(z, cosf, sinf, s0, lgt, nw, *, nb, t_len, tc, lb, chunk, shared_pos, hb):
    nt = t_len // tc
    wb = hb * RET_DK
    col0 = 2 * LRU_W // wb
    ngrp = RET_H // hb
    pos_map = (lambda b, h, t: (0, 0)) if shared_pos else (lambda b, h, t: (b * nt + t, 0))

    def zspec(off):
        return pl.BlockSpec((tc, wb), lambda b, h, t: (b * nt + t, col0 + off * ngrp + h))

    return pl.pallas_call(
        functools.partial(_retention_body, lb=lb, nsub=tc // lb, nt=nt, chunk=chunk, hb=hb),
        grid=(nb, ngrp, nt),
        in_specs=[
            zspec(0), zspec(1), zspec(2), zspec(3),
            pl.BlockSpec((tc, RET_DK), pos_map),
            pl.BlockSpec((tc, RET_DK), pos_map),
            pl.BlockSpec((1, hb, RET_DK, RET_DK), lambda b, h, t: (b, h, 0, 0)),
            pl.BlockSpec((hb, 1, RET_DK), lambda b, h, t: (h, 0, 0)),
            pl.BlockSpec((1, wb), lambda b, h, t: (0, h)),
        ],
        out_specs=[
            pl.BlockSpec((tc, wb), lambda b, h, t: (b * nt + t, h)),
            pl.BlockSpec((1, hb, RET_DK, RET_DK), lambda b, h, t: (b, h, 0, 0)),
        ],
        out_shape=[
            jax.ShapeDtypeStruct((nb * t_len, RET_H * RET_DK), BF16),
            jax.ShapeDtypeStruct((nb, RET_H, RET_DK, RET_DK), F32),
        ],
        scratch_shapes=[pltpu.VMEM((hb, RET_DK, RET_DK), F32)],
        compiler_params=_params(3),
        name="retention",
    )(z, z, z, z, cosf, sinf, s0, lgt, nw)


def _hgrn_body(q_ref, f_ref, v_ref, g_ref, lbl_ref, nw_ref, s0_ref, y_ref, so_ref, st_ref,
               *, chunk, nsub, nt, layer, hb):
    t = pl.program_id(2)
    L = chunk

    @pl.when(t == 0)
    def _():
        for h in range(hb):
            st_ref[h] = s0_ref[0, h].T

    lgt = lbl_ref[...]
    mx = jnp.max(lgt, axis=0, keepdims=True)
    ex = jnp.exp(lgt - mx)
    p = ex / jnp.sum(ex, axis=0, keepdims=True)
    lb_all = jnp.sum(p[0:layer + 1, :], axis=0, keepdims=True) - p[0:1, :]

    rowi = lax.broadcasted_iota(jnp.int32, (L, HG_DK), 0)
    ti = lax.broadcasted_iota(jnp.int32, (L, L), 0)
    si = lax.broadcasted_iota(jnp.int32, (L, L), 1)
    txs = ti ^ si
    below = si < ti
    eye = ti == si
    level_mask = {}
    blk = L
    while blk >= 2:
        level_mask[blk] = below & (txs >= blk // 2) & (txs < blk)
        blk //= 2
    upper = {}
    sign = {}
    blk = L
    while blk >= SUBLANES:
        upper[blk] = (rowi & (blk - 1)) >= blk // 2
        sign[blk] = jnp.where(upper[blk], 1.0, -1.0)
        blk //= 2
    i4 = rowi & 3
    odd = (rowi & 1) == 1
    row8 = lax.broadcasted_iota(jnp.int32, (SUBLANES, HG_DK), 0)
    nreg = L // SUBLANES
    scale = HG_DK ** -0.5

    def gram(x):
        xb = x.astype(BF16)
        return lax.dot_general(xb, xb, NT_DIMS, preferred_element_type=F32)

    def per_reg(x, fn):
        return jnp.concatenate([fn(x[i * SUBLANES:(i + 1) * SUBLANES, :]) for i in range(nreg)], axis=0)

    def running_sum(x):
        parts = [x[i * SUBLANES:(i + 1) * SUBLANES, :] for i in range(nreg)]
        for s in (1, 2, 4):
            parts = [p + jnp.where(row8 >= s, pltpu.roll(p, s, 0), 0.0) for p in parts]
        out = []
        carry = None
        for p in parts:
            if carry is not None:
                p = p + carry
            out.append(p)
            carry = p[SUBLANES - 1:SUBLANES, :]
        return jnp.concatenate(out, axis=0)

    def one_head(rows, h):
        cols = slice(h * HG_DK, (h + 1) * HG_DK)
        lbv = lb_all[:, cols]
        f = lbv + (1.0 - lbv) * _sigmoid(f_ref[rows, cols])
        qh = _silu(q_ref[rows, cols], scale)
        kk = 1.0 - f
        v = v_ref[rows, cols].astype(BF16)
        c = running_sum(jnp.log2(f))
        clast = c[L - 1:L, :]

        att = jnp.where(eye, jnp.sum(qh * kk, axis=-1, keepdims=True), 0.0)
        blk = L
        while blk >= SUBLANES:
            half = blk // 2
            r = jnp.concatenate(
                [jnp.broadcast_to(c[i * blk + half - 1:i * blk + half, :], (blk, HG_DK))
                 for i in range(L // blk)], axis=0)
            x = jnp.where(upper[blk], qh, kk) * jnp.exp2((c - r) * sign[blk])
            att = jnp.where(level_mask[blk], gram(x), att)
            blk = half
        fprev = per_reg(f, lambda p: pltpu.roll(p, 1, 0))
        fnext = per_reg(f, lambda p: pltpu.roll(p, SUBLANES - 1, 0))
        qf = qh * f
        x4 = jnp.where(i4 == 0, kk * fnext, jnp.where(i4 == 1, kk, jnp.where(i4 == 2, qf, qf * fprev)))
        att = jnp.where(level_mask[4], gram(x4), att)
        att = jnp.where(level_mask[2], gram(jnp.where(odd, qf, kk)), att)
        qe = (qh * jnp.exp2(c)).astype(BF16)
        kd = (kk * jnp.exp2(clast - c)).astype(BF16)
        return att.astype(BF16), qe, kd, v, jnp.exp2(clast)

    def head_state(rows, h, att, qe, kd, v, dec):
        cols = slice(h * HG_DK, (h + 1) * HG_DK)
        st = st_ref[h]
        o = jnp.dot(att, v, preferred_element_type=F32)
        o = o + lax.dot_general(qe, st.astype(BF16), NT_DIMS, preferred_element_type=F32)
        st_ref[h] = dec * st + lax.dot_general(v, kd, TN_DIMS, preferred_element_type=F32)
        yn = o * lax.rsqrt(jnp.mean(o * o, axis=-1, keepdims=True) + EPS)
        y_ref[rows, cols] = (yn * nw_ref[:, cols] * _silu(g_ref[rows, cols])).astype(y_ref.dtype)

    def sub(j, _):
        rows = pl.ds(pl.multiple_of(j * L, L), L)
        staged = [one_head(rows, h) for h in range(hb)]
        for h in range(hb):
            head_state(rows, h, *staged[h])
        return 0

    lax.fori_loop(0, nsub, sub, 0)

    @pl.when(t == nt - 1)
    def _():
        for h in range(hb):
            so_ref[0, h] = st_ref[h].T


def _hgrn(z, logits, nw, s0, *, nb, t_len, tc, chunk, layer, hb):
    nt = t_len // tc
    wb = hb * HG_DK
    ngrp = HG_H // hb

    def zspec(off):
        return pl.BlockSpec((tc, wb), lambda b, h, t: (b * nt + t, off * ngrp + h))

    return pl.pallas_call(
        functools.partial(_hgrn_body, chunk=chunk, nsub=tc // chunk, nt=nt, layer=layer, hb=hb),
        grid=(nb, ngrp, nt),
        in_specs=[
            zspec(0), zspec(1), zspec(2), zspec(3),
            pl.BlockSpec((logits.shape[0], wb), lambda b, h, t: (0, h)),
            pl.BlockSpec((1, wb), lambda b, h, t: (0, h)),
            pl.BlockSpec((1, hb, HG_DK, HG_DK), lambda b, h, t: (b, h, 0, 0)),
        ],
        out_specs=[
            pl.BlockSpec((tc, wb), lambda b, h, t: (b * nt + t, h)),
            pl.BlockSpec((1, hb, HG_DK, HG_DK), lambda b, h, t: (b, h, 0, 0)),
        ],
        out_shape=[
            jax.ShapeDtypeStruct((nb * t_len, HG_H * HG_DK), BF16),
            jax.ShapeDtypeStruct((nb, HG_H, HG_DK, HG_DK), F32),
        ],
        scratch_shapes=[pltpu.VMEM((hb, HG_DK, HG_DK), F32)],
        compiler_params=_params(3),
        name="hgrn",
    )(z, z, z, z, logits, nw, s0)


def _rope_tables(pos):
    half = RET_DK // 2
    inv = ROPE_BASE ** (-jnp.arange(half, dtype=F32) / half)
    ang = pos[:, None] * inv[None, :]
    cos = jnp.cos(ang)
    sin = jnp.sin(ang)
    return jnp.concatenate([cos, cos], axis=-1), jnp.concatenate([-sin, sin], axis=-1)


def _trunk(x, conv0, lru0, ret0, hg0, pos0, w, *, nb, t_len, tm, tn, tb, ret_tc, ret_lb, ret_hb,
           hg_tc, hg_hb):
    chunk = min(t_len, CHUNK)
    cosf, sinf = _rope_tables(pos0 + jnp.arange(t_len, dtype=F32))
    log_g = jnp.log1p(-jnp.exp2(-5.0 - jnp.arange(RET_H, dtype=F32)))
    lgt = jnp.broadcast_to(log_g[:, None, None], (RET_H, 1, RET_DK))
    row = lambda a: a.reshape(1, -1)

    x = _ffn(x, row(w["ffn1_norm"][0]), w["ffn1_wg"], w["ffn1_wu"], w["ffn1_wd"],
             row(w["final_norm"]), layer=0, tm=tm, tf=512, final=False)
    z = _normproj(x, row(w["mix_norm"][0]), w["ab_w_in"][0], tm=tm, tn=tn)
    ya, nbuf, nh = _rglru(z, conv0, lru0.reshape(nb, 1, LRU_W), w["ab_conv_w"][0], row(w["ab_conv_b"][0]),
                          w["ab_gate_a_w"][0], row(w["ab_gate_a_b"][0]), w["ab_gate_x_w"][0],
                          row(w["ab_gate_x_b"][0]), row(w["ab_lru_lambda"][0]),
                          nb=nb, t_len=t_len, tb=tb)
    yr, ns = _retention(z, cosf, sinf, ret0, lgt, row(w["ab_ret_norm"][0]), nb=nb, t_len=t_len,
                        tc=ret_tc, lb=ret_lb, chunk=chunk, shared_pos=nb > 1, hb=ret_hb)
    x = _projout([ya, yr], w["ab_w_out"][0], x, tm=min(tm, 512))
    x = _ffn(x, row(w["ffn2_norm"][0]), w["ffn2_wg"], w["ffn2_wu"], w["ffn2_wd"],
             row(w["final_norm"]), layer=0, tm=tm, tf=512, final=False)

    x = _ffn(x, row(w["ffn1_norm"][1]), w["ffn1_wg"], w["ffn1_wu"], w["ffn1_wd"],
             row(w["final_norm"]), layer=1, tm=tm, tf=512, final=False)
    z = _normproj(x, row(w["mix_norm"][1]), w["c_w_in"][0], tm=tm, tn=tn)
    yc, nhg = _hgrn(z, w["c_lb_logits"], row(w["c_norm"][0]), hg0, nb=nb, t_len=t_len, tc=hg_tc,
                    chunk=chunk, layer=1, hb=hg_hb)
    x = _projout([yc], w["c_w_out"][0], x, tm=min(tm, 512))
    y = _ffn(x, row(w["ffn2_norm"][1]), w["ffn2_wg"], w["ffn2_wu"], w["ffn2_wd"],
             row(w["final_norm"]), layer=1, tm=tm, tf=512, final=True)
    return (y.reshape(nb, t_len, D_MODEL), nbuf[None], nh.reshape(1, nb, LRU_W), ns[None], nhg[None])


_MATMUL_WEIGHTS = ("ffn1_wg", "ffn1_wu", "ffn1_wd", "ffn2_wg", "ffn2_wu", "ffn2_wd",
                   "ab_w_in", "ab_gate_a_w", "ab_gate_x_w", "ab_w_out", "c_w_in", "c_w_out")


def kernel(x_prompt, x_sample, state_conv, state_lru, state_ret, state_hgrn, ffn1_norm, ffn1_wg, ffn1_wu, ffn1_wd, mix_norm, ffn2_norm, ffn2_wg, ffn2_wu, ffn2_wd, final_norm, ab_w_in, ab_conv_w, ab_conv_b, ab_gate_a_w, ab_gate_a_b, ab_gate_x_w, ab_gate_x_b, ab_lru_lambda, ab_ret_norm, ab_w_out, c_w_in, c_lb_logits, c_norm, c_w_out):
    w = dict(ffn1_norm=ffn1_norm, ffn1_wg=ffn1_wg, ffn1_wu=ffn1_wu, ffn1_wd=ffn1_wd, mix_norm=mix_norm,
             ffn2_norm=ffn2_norm, ffn2_wg=ffn2_wg, ffn2_wu=ffn2_wu, ffn2_wd=ffn2_wd,
             final_norm=final_norm, ab_w_in=ab_w_in, ab_conv_w=ab_conv_w, ab_conv_b=ab_conv_b,
             ab_gate_a_w=ab_gate_a_w, ab_gate_a_b=ab_gate_a_b, ab_gate_x_w=ab_gate_x_w,
             ab_gate_x_b=ab_gate_x_b, ab_lru_lambda=ab_lru_lambda, ab_ret_norm=ab_ret_norm,
             ab_w_out=ab_w_out, c_w_in=c_w_in, c_lb_logits=c_lb_logits, c_norm=c_norm, c_w_out=c_w_out)
    for name in _MATMUL_WEIGHTS:
        w[name] = w[name].astype(BF16)

    nbp, tp, _ = x_prompt.shape
    nbs, ts, _ = x_sample.shape
    zc = jnp.zeros((nbp, CONV_W - 1, LRU_W), F32)
    zl = jnp.zeros((nbp, LRU_W), F32)
    zr = jnp.zeros((nbp, RET_H, RET_DK, RET_DK), F32)
    zh = jnp.zeros((nbp, HG_H, HG_DK, HG_DK), F32)
    yp, cp, lp, rp, hp = _trunk(x_prompt.reshape(nbp * tp, D_MODEL), zc, zl, zr, zh, 0.0, w,
                                nb=nbp, t_len=tp, tm=1024, tn=1024, tb=512, ret_tc=1024, ret_lb=256, ret_hb=8,
                                hg_tc=512, hg_hb=16)
    ys, cs, ls, rs, hs = _trunk(x_sample.reshape(nbs * ts, D_MODEL), state_conv[0], state_lru[0],
                                state_ret[0], state_hgrn[0], float(PAST_LEN), w,
                                nb=nbs, t_len=ts, tm=nbs * ts, tn=1024, tb=ts, ret_tc=ts, ret_lb=ts, ret_hb=8,
                                hg_tc=ts, hg_hb=8)
    return (yp, ys, cp, lp, rp, hp, cs, ls, rs, hs)
```
